```python
import math
import jax, jax.numpy as jnp
from jax import lax
import numpy as np

D_MODEL = 1024
BATCH = 8
SEQ = 4096
DEPTH = 2

D_MIX = D_MODEL
MLA_HEADS = 8
QK_NOPE_DIM = 64
QK_ROPE_DIM = 32
V_HEAD_DIM = 64
Q_LORA_RANK = 256
KV_LORA_RANK = 128
MLA_WIDTH = MLA_HEADS * V_HEAD_DIM
ROPE_THETA = 10000.0
Q_BLOCK = 128
SSM_WIDTH = D_MIX - MLA_WIDTH
SSM_GROUP = 16
SSM_GROUPS = SSM_WIDTH // SSM_GROUP
SSM_STATE = 64
DT_MIN = 0.001
DT_MAX = 0.1
IN_WIDTH = Q_LORA_RANK + KV_LORA_RANK + QK_ROPE_DIM + SSM_WIDTH
MEM_LEN = 256
X_HEADS = 4
X_HEAD_DIM = D_MODEL // X_HEADS
D_FF = -(-8 * D_MODEL // (3 * 256)) * 256
EPS = 1e-6

kernel_name = 'hybrid_mla_s5_memory_decoder'


def rmsnorm(x, g):
    xf = x.astype(jnp.float32)
    y = xf * lax.rsqrt(jnp.mean(xf * xf, axis=-1, keepdims=True) + EPS)
    return (y * g.astype(jnp.float32)).astype(x.dtype)


def apply_rope(x, cos, sin):
    xf = x.astype(jnp.float32)
    half = xf.shape[-1] // 2
    x1, x2 = xf[..., :half], xf[..., half:]
    return jnp.concatenate([x1 * cos - x2 * sin, x2 * cos + x1 * sin], axis=-1).astype(x.dtype)


def mla_group(c_q, c_kv, k_r, q_norm_g, w_uq, kv_norm_g, w_ukv, cos, sin):
    B, S, _ = c_q.shape
    q = (rmsnorm(c_q, q_norm_g) @ w_uq).reshape(B, S, MLA_HEADS, QK_NOPE_DIM + QK_ROPE_DIM)
    q_nope = q[..., :QK_NOPE_DIM]
    q_rope = apply_rope(q[..., QK_NOPE_DIM:], cos[:, :, None, :], sin[:, :, None, :])
    kv = (rmsnorm(c_kv, kv_norm_g) @ w_ukv).reshape(B, S, MLA_HEADS, QK_NOPE_DIM + V_HEAD_DIM)
    k_nope = kv[..., :QK_NOPE_DIM].transpose(0, 2, 1, 3)
    v = kv[..., QK_NOPE_DIM:].transpose(0, 2, 1, 3)
    k_rope = apply_rope(k_r, cos, sin)
    scale = (QK_NOPE_DIM + QK_ROPE_DIM) ** -0.5
    qn = (q_nope * scale).transpose(0, 2, 1, 3)
    qr = (q_rope * scale).transpose(0, 2, 1, 3)
    outs = []
    for i in range(S // Q_BLOCK):
        lo, hi = i * Q_BLOCK, (i + 1) * Q_BLOCK
        s = (jnp.einsum('bhqd,bhkd->bhqk', qn[:, :, lo:hi], k_nope[:, :, :hi])
             + jnp.einsum('bhqr,bkr->bhqk', qr[:, :, lo:hi], k_rope[:, :hi])).astype(jnp.float32)
        mask = jnp.arange(hi)[None, :] <= jnp.arange(lo, hi)[:, None]
        s = jnp.where(mask, s, -jnp.inf)
        p = jax.nn.softmax(s, axis=-1).astype(v.dtype)
        outs.append(jnp.einsum('bhqk,bhkd->bqhd', p, v[:, :, :hi]))
    o = jnp.concatenate(outs, axis=1)
    return o.reshape(B, S, MLA_WIDTH)


def s5_group(u, lam_re, lam_im, log_dt, b_re, b_im, c_re, c_im, d, w_glu, b_glu):
    B, S, _ = u.shape
    f32 = jnp.float32
    uf = u.astype(f32)
    ug = uf.reshape(B, S, SSM_GROUPS, SSM_GROUP)
    lam = lax.complex(lam_re.astype(f32), lam_im.astype(f32))
    dt = jnp.exp(log_dt.astype(f32))[:, None]
    a_bar = jnp.exp(lam * dt)
    b_mat = lax.complex(b_re.astype(f32), b_im.astype(f32))
    b_bar = ((a_bar - 1.0) / lam)[..., None] * b_mat
    bu = jnp.einsum('bsgc,gpc->bsgp', ug.astype(jnp.complex64), b_bar)
    a_elems = jnp.broadcast_to(a_bar, bu.shape)

    def combine(e1, e2):
        a1, x1 = e1
        a2, x2 = e2
        return a1 * a2, a2 * x1 + x2

    _, states = lax.associative_scan(combine, (a_elems, bu), axis=1)
    c_mat = lax.complex(c_re.astype(f32), c_im.astype(f32))
    y = jnp.einsum('bsgp,gcp->bsgc', states, c_mat).real.reshape(B, S, SSM_WIDTH)
    y = y + d.astype(f32) * uf
    g = jax.nn.gelu(y)
    y = y * jax.nn.sigmoid(g @ w_glu.astype(f32) + b_glu.astype(f32))
    return y.astype(u.dtype)


def memory_cross_attention(hn, memn, w_xq, w_xkv, w_xo):
    B, S, _ = hn.shape
    M = memn.shape[1]
    q = (hn @ w_xq).reshape(B, S, X_HEADS, X_HEAD_DIM)
    kv = (memn @ w_xkv).reshape(B, M, 2, X_HEADS, X_HEAD_DIM)
    k, v = kv[:, :, 0], kv[:, :, 1]
    s = jnp.einsum('bshd,bmhd->bhsm', q, k).astype(jnp.float32) * (X_HEAD_DIM ** -0.5)
    p = jax.nn.softmax(s, axis=-1).astype(v.dtype)
    o = jnp.einsum('bhsm,bmhd->bshd', p, v).reshape(B, S, D_MODEL)
    return o @ w_xo


def swiglu(hn, w_gate, w_up, w_down):
    return (jax.nn.silu(hn @ w_gate) * (hn @ w_up)) @ w_down


def setup_inputs(seed: int = 0) -> dict:
    key = jax.random.key(seed)
    ks = jax.random.split(key, 40)
    f32 = jnp.float32

    def nrm(k, shape, fan_in):
        return jax.random.normal(k, shape, f32) * (fan_in ** -0.5)

    def gain(k, shape):
        return 1.0 + 0.05 * jax.random.normal(k, shape, f32)

    L = DEPTH
    x = jax.random.normal(ks[0], (BATCH, SEQ, D_MODEL), f32)
    mem = jax.random.normal(ks[1], (BATCH, MEM_LEN, D_MODEL), f32)
    start = jax.random.randint(ks[2], (BATCH, 1), 0, 1024, dtype=jnp.int32)
    positions = start + jnp.arange(SEQ, dtype=jnp.int32)[None, :]
    n_idx = jnp.arange(SSM_STATE, dtype=f32)
    ssm_lambda_re = -0.5 * jnp.exp(0.05 * jax.random.normal(ks[9], (L, SSM_GROUPS, SSM_STATE), f32))
    ssm_lambda_im = jnp.pi * n_idx + 0.01 * jax.random.normal(ks[10], (L, SSM_GROUPS, SSM_STATE), f32)
    ssm_log_dt = jax.random.uniform(ks[11], (L, SSM_GROUPS), f32, math.log(DT_MIN), math.log(DT_MAX))
    return {
        'x': x,
        'mem': mem,
        'positions': positions,
        'norm_mix_g': gain(ks[3], (L, D_MODEL)),
        'w_in': nrm(ks[4], (L, D_MODEL, IN_WIDTH), D_MODEL),
        'q_norm_g': gain(ks[5], (L, Q_LORA_RANK)),
        'w_uq': nrm(ks[6], (L, Q_LORA_RANK, MLA_HEADS * (QK_NOPE_DIM + QK_ROPE_DIM)), Q_LORA_RANK),
        'kv_norm_g': gain(ks[7], (L, KV_LORA_RANK)),
        'w_ukv': nrm(ks[8], (L, KV_LORA_RANK, MLA_HEADS * (QK_NOPE_DIM + V_HEAD_DIM)), KV_LORA_RANK),
        'ssm_lambda_re': ssm_lambda_re,
        'ssm_lambda_im': ssm_lambda_im,
        'ssm_log_dt': ssm_log_dt,
        'ssm_b_re': nrm(ks[12], (L, SSM_GROUPS, SSM_STATE, SSM_GROUP), 2 * SSM_GROUP),
        'ssm_b_im': nrm(ks[13], (L, SSM_GROUPS, SSM_STATE, SSM_GROUP), 2 * SSM_GROUP),
        'ssm_c_re': nrm(ks[14], (L, SSM_GROUPS, SSM_GROUP, SSM_STATE), 2 * SSM_STATE),
        'ssm_c_im': nrm(ks[15], (L, SSM_GROUPS, SSM_GROUP, SSM_STATE), 2 * SSM_STATE),
        'ssm_d': jax.random.normal(ks[16], (L, SSM_WIDTH), f32),
        'ssm_w_glu': nrm(ks[17], (L, SSM_WIDTH, SSM_WIDTH), SSM_WIDTH),
        'ssm_b_glu': 0.01 * jax.random.normal(ks[18], (L, SSM_WIDTH), f32),
        'attn_out_g': gain(ks[19], (L, MLA_WIDTH)),
        'ssm_out_g': gain(ks[20], (L, SSM_WIDTH)),
        'w_out': nrm(ks[21], (L, D_MIX, D_MODEL), D_MIX),
        'norm_x_g': gain(ks[22], (L, D_MODEL)),
        'mem_norm_g': gain(ks[23], (L, D_MODEL)),
        'w_xq': nrm(ks[24], (L, D_MODEL, D_MODEL), D_MODEL),
        'w_xkv': nrm(ks[25], (L, D_MODEL, 2 * D_MODEL), D_MODEL),
        'w_xo': nrm(ks[26], (L, D_MODEL, D_MODEL), D_MODEL),
        'norm_ffn_g': gain(ks[27], (L, D_MODEL)),
        'w_gate': nrm(ks[28], (L, D_MODEL, D_FF), D_MODEL),
        'w_up': nrm(ks[29], (L, D_MODEL, D_FF), D_MODEL),
        'w_down': nrm(ks[30], (L, D_FF, D_MODEL), D_FF),
        'final_norm_g': gain(ks[31], (D_MODEL,)),
    }


def reference(x, mem, positions, norm_mix_g, w_in, q_norm_g, w_uq, kv_norm_g, w_ukv,
              ssm_lambda_re, ssm_lambda_im, ssm_log_dt, ssm_b_re, ssm_b_im, ssm_c_re, ssm_c_im,
              ssm_d, ssm_w_glu, ssm_b_glu, attn_out_g, ssm_out_g, w_out, norm_x_g, mem_norm_g,
              w_xq, w_xkv, w_xo, norm_ffn_g, w_gate, w_up, w_down, final_norm_g):
    freqs = ROPE_THETA ** (-jnp.arange(0, QK_ROPE_DIM, 2, dtype=jnp.float32) / QK_ROPE_DIM)
    ang = positions.astype(jnp.float32)[..., None] * freqs
    cos, sin = jnp.cos(ang), jnp.sin(ang)
    split_at = [Q_LORA_RANK, Q_LORA_RANK + KV_LORA_RANK, Q_LORA_RANK + KV_LORA_RANK + QK_ROPE_DIM]
    h = x
    for l in range(DEPTH):
        xn = rmsnorm(h, norm_mix_g[l])
        proj = xn @ w_in[l]
        c_q, c_kv, k_r, u = jnp.split(proj, split_at, axis=-1)
        a_out = mla_group(c_q, c_kv, k_r, q_norm_g[l], w_uq[l], kv_norm_g[l], w_ukv[l], cos, sin)
        s_out = s5_group(u, ssm_lambda_re[l], ssm_lambda_im[l], ssm_log_dt[l], ssm_b_re[l], ssm_b_im[l],
                         ssm_c_re[l], ssm_c_im[l], ssm_d[l], ssm_w_glu[l], ssm_b_glu[l])
        mixed = jnp.concatenate([rmsnorm(a_out, attn_out_g[l]), rmsnorm(s_out, ssm_out_g[l])], axis=-1)
        h = h + mixed @ w_out[l]
        h = h + memory_cross_attention(rmsnorm(h, norm_x_g[l]), rmsnorm(mem, mem_norm_g[l]),
                                       w_xq[l], w_xkv[l], w_xo[l])
        h = h + swiglu(rmsnorm(h, norm_ffn_g[l]), w_gate[l], w_up[l], w_down[l])
    return rmsnorm(h, final_norm_g)
```

```python
import functools
import math

import jax
import jax.numpy as jnp
from jax import lax
from jax.experimental import pallas as pl
from jax.experimental.pallas import tpu as pltpu

LANES = 128
VMEM_LIMIT_BYTES = 56 * 1024 * 1024

EPS = 1e-6
HEADS = 8
NOPE = 64
ROPE = 32
VDIM = 64
HEAD_PAD = LANES
Q_LORA = 256
KV_LORA = 128
ROPE_THETA = 10000.0
SSM_GROUP = 16
SSM_STATE = 64
CHUNK = LANES
X_HEADS = 4

F32 = jnp.float32
BF16 = jnp.bfloat16
NEG = -1e30


def _dot(a, b):
    return jnp.dot(a, b, preferred_element_type=F32)


def _dot_nt(a, b):
    return lax.dot_general(a, b, (((1,), (1,)), ((), ())), preferred_element_type=F32)


def _rms_rows(x, g):
    return x * lax.rsqrt(jnp.mean(x * x, axis=-1, keepdims=True) + EPS) * g


def _params(*sem):
    return pltpu.CompilerParams(dimension_semantics=sem, vmem_limit_bytes=VMEM_LIMIT_BYTES)


def _const_spec(shape):
    nd = len(shape)
    return pl.BlockSpec(shape, lambda *_: (0,) * nd)


def _front_kernel(h_ref, pos_ref, freq_ref, gmix_ref, wcat_ref, wut_ref, qg_ref, wqm_ref, wqs_ref,
                  kvg_ref, wk_ref, wv_ref, q_ref, k_ref, v_ref, ut_ref):
    h = h_ref[0]
    xn = _rms_rows(h, gmix_ref[...]).astype(BF16)
    proj = _dot(xn, wcat_ref[...])
    cq = proj[:, :Q_LORA]
    ckv = proj[:, Q_LORA:Q_LORA + KV_LORA]
    kr_main = proj[:, Q_LORA + KV_LORA:Q_LORA + KV_LORA + LANES]
    kr_swap = proj[:, Q_LORA + KV_LORA + LANES:]
    ang = pos_ref[0].astype(F32) * freq_ref[...]
    cos = jnp.cos(ang)
    sin = jnp.sin(ang)
    scale = (NOPE + ROPE) ** -0.5
    cqn = _rms_rows(cq, qg_ref[...]).astype(BF16)
    ckvn = _rms_rows(ckv, kvg_ref[...]).astype(BF16)
    qm = _dot(cqn, wqm_ref[...])
    qs = _dot(cqn, wqs_ref[...])
    cos_q = jnp.tile(cos * scale, (1, HEADS))
    sin_q = jnp.tile(sin * scale, (1, HEADS))
    q_ref[0] = (qm * cos_q + qs * sin_q).astype(BF16)
    k_rot = kr_main * cos + kr_swap * sin
    k_ref[0] = (_dot(ckvn, wk_ref[...]) + jnp.tile(k_rot, (1, HEADS))).astype(BF16)
    v_ref[0] = _dot(ckvn, wv_ref[...]).astype(BF16)
    ut_ref[0] = _dot_nt(wut_ref[...], xn).astype(BF16)


def _front(h, pos3, freq, gmix, wcat, wut, qg, wqm, wqs, kvg, wk, wv, *, tm):
    B, S, D = h.shape
    n_ssm = wut.shape[0]
    grid = (B, S // tm)
    row = lambda b, i: (b, i, 0)
    in_specs = [
        pl.BlockSpec((1, tm, D), row),
        pl.BlockSpec((1, tm, 1), row),
        _const_spec(freq.shape), _const_spec(gmix.shape), _const_spec(wcat.shape), _const_spec(wut.shape),
        _const_spec(qg.shape), _const_spec(wqm.shape), _const_spec(wqs.shape),
        _const_spec(kvg.shape), _const_spec(wk.shape), _const_spec(wv.shape),
    ]
    out_specs = [
        pl.BlockSpec((1, tm, HEADS * HEAD_PAD), row),
        pl.BlockSpec((1, tm, HEADS * HEAD_PAD), row),
        pl.BlockSpec((1, tm, HEADS * VDIM), row),
        pl.BlockSpec((1, n_ssm, tm), lambda b, i: (b, 0, i)),
    ]
    out_shape = [
        jax.ShapeDtypeStruct((B, S, HEADS * HEAD_PAD), BF16),
        jax.ShapeDtypeStruct((B, S, HEADS * HEAD_PAD), BF16),
        jax.ShapeDtypeStruct((B, S, HEADS * VDIM), BF16),
        jax.ShapeDtypeStruct((B, n_ssm, S), BF16),
    ]
    return pl.pallas_call(
        _front_kernel, grid=grid, in_specs=in_specs, out_specs=out_specs, out_shape=out_shape,
        compiler_params=_params("parallel", "parallel"), name="front",
    )(h, pos3, freq, gmix, wcat, wut, qg, wqm, wqs, kvg, wk, wv)


def _mla_kernel(q_ref, k_ref, v_ref, o_ref, *, tq):
    qi = pl.program_id(2)
    row = lax.broadcasted_iota(jnp.int32, (tq, tq), 0)
    col = lax.broadcasted_iota(jnp.int32, (tq, tq), 1)
    causal = col <= row
    heads = []
    for hh in range(2):
        q = q_ref[0, :, hh * HEAD_PAD:(hh + 1) * HEAD_PAD]

        def step(j, carry, masked):
            m, l, acc = carry
            start = pl.multiple_of(j * tq, tq)
            kc = k_ref[0, pl.ds(start, tq), hh * HEAD_PAD:(hh + 1) * HEAD_PAD]
            vc = v_ref[0, pl.ds(start, tq), :]
            s = _dot_nt(q, kc)
            if masked:
                s = jnp.where(causal, s, NEG)
            m_new = jnp.maximum(m, jnp.max(s, axis=-1, keepdims=True))
            alpha = jnp.exp(m - m_new)
            p = jnp.exp(s - m_new)
            l = alpha * l + jnp.sum(p, axis=-1, keepdims=True)
            acc = alpha * acc + _dot(p.astype(BF16), vc)
            return m_new, l, acc

        carry = (jnp.full((tq, 1), NEG, F32), jnp.zeros((tq, 1), F32), jnp.zeros((tq, 2 * VDIM), F32))
        carry = lax.fori_loop(0, qi, functools.partial(step, masked=False), carry)
        _, l, acc = step(qi, carry, True)
        heads.append(acc / l)
    lane = lax.broadcasted_iota(jnp.int32, (tq, 2 * VDIM), 1)
    o_ref[0] = jnp.where(lane < VDIM, heads[0], heads[1]).astype(o_ref.dtype)


def _mla(q, k, v, *, tq):
    B, S, _ = q.shape
    grid = (B, HEADS // 2, S // tq)
    return pl.pallas_call(
        functools.partial(_mla_kernel, tq=tq), grid=grid,
        in_specs=[
            pl.BlockSpec((1, tq, 2 * HEAD_PAD), lambda b, p, i: (b, i, p)),
            pl.BlockSpec((1, S, 2 * HEAD_PAD), lambda b, p, i: (b, 0, p)),
            pl.BlockSpec((1, S, 2 * VDIM), lambda b, p, i: (b, 0, p)),
        ],
        out_specs=pl.BlockSpec((1, tq, 2 * VDIM), lambda b, p, i: (b, i, p)),
        out_shape=jax.ShapeDtypeStruct((B, S, HEADS * VDIM), BF16),
        compiler_params=_params("parallel", "parallel", "arbitrary"), name="mla",
    )(q, k, v)


def _cpow(zr, zi, e):
    mag = jnp.exp(e * zr)
    return mag * jnp.cos(e * zi), mag * jnp.sin(e * zi)


def _s5_kernel(u_ref, ldt_ref, lrr_ref, lir_ref, lrc_ref, lic_ref, bx_ref, by_ref, cx_ref, cy_ref,
               y_ref, tt_ref, wst_ref, wout_ref, ep_ref, km_ref, lhs_ref, *, nb, nk):
    P2 = 2 * SSM_STATE
    G = SSM_GROUP
    dt = jnp.exp(ldt_ref[0])
    lr = lrr_ref[0]
    li = lir_ref[0]
    zr = lr * dt
    zi = li * dt
    ar, ai = _cpow(zr, zi, 1.0)
    den = lr * lr + li * li
    cr = ((ar - 1.0) * lr + ai * li) / den
    ci = (ai * lr - (ar - 1.0) * li) / den
    bx = bx_ref[0]
    by = by_ref[0]
    bbx = cr * bx + ci * by
    bby = cr * by - ci * bx
    lane = lax.broadcasted_iota(jnp.int32, (1, P2), 1)
    sign_lane = jnp.where(lane < SSM_STATE, 1.0, -1.0).astype(F32)
    e_rows = (CHUNK - 1 - lax.broadcasted_iota(jnp.int32, (CHUNK, 1), 0)).astype(F32)
    pr, pi = _cpow(zr, zi, e_rows)
    for c in range(G):
        wst_ref[c * CHUNK:(c + 1) * CHUNK, :] = (pr * bbx[c:c + 1, :] + pi * bby[c:c + 1, :]).astype(BF16)
    zrc = lrc_ref[0] * dt
    zic = lic_ref[0] * dt
    tau = lax.broadcasted_iota(jnp.int32, (1, CHUNK), 1).astype(F32)
    a0r, a0i = _cpow(zrc, zic, tau)
    a1r, a1i = _cpow(zrc, zic, tau + 1.0)
    cx = cx_ref[0]
    cy = cy_ref[0]
    sub = lax.broadcasted_iota(jnp.int32, (P2, 1), 0)
    sign_sub = jnp.where(sub < SSM_STATE, 1.0, -1.0).astype(F32)
    for c in range(G):
        cxc = cx[:, c:c + 1]
        cyc = cy[:, c:c + 1]
        ep_ref[:, c * CHUNK:(c + 1) * CHUNK] = a0r * cxc + a0i * cyc
        wout_ref[:, c * CHUNK:(c + 1) * CHUNK] = ((a1r * cxc + a1i * cyc) * sign_sub).astype(BF16)
    km_ref[...] = jnp.dot(bbx * sign_lane, ep_ref[...], preferred_element_type=F32,
                          precision=lax.Precision.HIGHEST)
    jj = lax.broadcasted_iota(jnp.int32, (CHUNK, CHUNK), 0)
    tcol = lax.broadcasted_iota(jnp.int32, (CHUNK, CHUNK), 1)
    lower = tcol >= jj

    def toeplitz_rows(cp, _):
        krow = km_ref[pl.ds(cp, 1), :]
        r0 = pl.multiple_of(cp * CHUNK, CHUNK)
        for c in range(G):
            blk = jnp.broadcast_to(krow[:, c * CHUNK:(c + 1) * CHUNK], (CHUNK, CHUNK))
            blk = pltpu.roll(blk, 0, 1, stride=1, stride_axis=0)
            tt_ref[pl.ds(r0, CHUNK), c * CHUNK:(c + 1) * CHUNK] = jnp.where(lower, blk, 0.0).astype(BF16)
        return 0

    lax.fori_loop(0, G, toeplitz_rows, 0)
    for b in range(nb):
        for c in range(G):
            lhs_ref[b * nk:(b + 1) * nk, c * CHUNK:(c + 1) * CHUNK] = u_ref[b, 0, c * nk:(c + 1) * nk, :]
    lhs = lhs_ref[...]
    y = _dot(lhs, tt_ref[...])
    x = _dot(lhs, wst_ref[...])
    kidx = lax.broadcasted_iota(jnp.int32, (nb * nk, 1), 0) % nk
    qr, qi = ar, ai
    for _ in range(int(math.log2(CHUNK))):
        qr, qi = qr * qr - qi * qi, 2.0 * qr * qi
    d = 1
    while d < nk:
        sx = jnp.where(kidx >= d, pltpu.roll(x, d, 0), 0.0)
        x = x + qr * sx + (qi * -sign_lane) * pltpu.roll(sx, SSM_STATE, 1)
        qr, qi = qr * qr - qi * qi, 2.0 * qr * qi
        d *= 2
    x0 = jnp.where(kidx >= 1, pltpu.roll(x, 1, 0), 0.0)
    y = y + _dot(x0.astype(BF16), wout_ref[...])
    for b in range(nb):
        for c in range(G):
            y_ref[b, 0, c * nk:(c + 1) * nk, :] = y[b * nk:(b + 1) * nk, c * CHUNK:(c + 1) * CHUNK]


def _s5(u4, ldt, lrr, lir, lrc, lic, bx, by, cx, cy):
    B, NG, R, _ = u4.shape
    nk = R // SSM_GROUP
    gspec = lambda shape: pl.BlockSpec((1,) + shape, lambda g: (g, 0, 0))
    ublk = pl.BlockSpec((B, 1, R, CHUNK), lambda g: (0, g, 0, 0))
    W = SSM_GROUP * CHUNK
    return pl.pallas_call(
        functools.partial(_s5_kernel, nb=B, nk=nk), grid=(NG,),
        in_specs=[ublk, gspec((1, 1)), gspec((1, 2 * SSM_STATE)), gspec((1, 2 * SSM_STATE)),
                  gspec((2 * SSM_STATE, 1)), gspec((2 * SSM_STATE, 1)),
                  gspec((SSM_GROUP, 2 * SSM_STATE)), gspec((SSM_GROUP, 2 * SSM_STATE)),
                  gspec((2 * SSM_STATE, SSM_GROUP)), gspec((2 * SSM_STATE, SSM_GROUP))],
        out_specs=ublk,
        out_shape=jax.ShapeDtypeStruct(u4.shape, F32),
        scratch_shapes=[
            pltpu.VMEM((W, W), BF16),
            pltpu.VMEM((W, 2 * SSM_STATE), BF16),
            pltpu.VMEM((2 * SSM_STATE, W), BF16),
            pltpu.VMEM((2 * SSM_STATE, W), F32),
            pltpu.VMEM((SSM_GROUP, W), F32),
            pltpu.VMEM((B * nk, W), BF16),
        ],
        compiler_params=_params("parallel"), name="s5",
    )(u4, ldt, lrr, lir, lrc, lic, bx, by, cx, cy)


def _memkv_kernel(mem_ref, g_ref, w_ref, k_ref, v_ref):
    mn = _rms_rows(mem_ref[0], g_ref[...]).astype(BF16)
    kv = _dot(mn, w_ref[...])
    d = k_ref.shape[-1]
    k_ref[0] = kv[:, :d].astype(BF16)
    v_ref[0] = kv[:, d:].astype(BF16)


def _memkv(mem, g, w):
    B, M, D = mem.shape
    blk = pl.BlockSpec((1, M, D), lambda b: (b, 0, 0))
    return pl.pallas_call(
        _memkv_kernel, grid=(B,),
        in_specs=[blk, _const_spec(g.shape), _const_spec(w.shape)],
        out_specs=[blk, blk],
        out_shape=[jax.ShapeDtypeStruct((B, M, D), BF16)] * 2,
        compiler_params=_params("parallel"), name="memkv",
    )(mem, g, w)


def _mix_kernel(h_ref, a_ref, yt_ref, ut_ref, d_ref, wglut_ref, bglu_ref, ag_ref, sg_ref, woa_ref, wos_ref,
                xg_ref, wxq_ref, kx_ref, vx_ref, wxo_ref, o_ref):
    y = yt_ref[0] + d_ref[...] * ut_ref[0].astype(F32)
    g = jax.nn.gelu(y)
    z = _dot(wglut_ref[...], g.astype(BF16)) + bglu_ref[...]
    s = y * jax.nn.sigmoid(z)
    s_n = s * lax.rsqrt(jnp.mean(s * s, axis=0, keepdims=True) + EPS) * sg_ref[...]
    a_n = _rms_rows(a_ref[0].astype(F32), ag_ref[...])
    h1 = h_ref[0] + _dot(a_n.astype(BF16), woa_ref[...]) + _dot(s_n.T.astype(BF16), wos_ref[...])
    qx = _dot(_rms_rows(h1, xg_ref[...]).astype(BF16), wxq_ref[...])
    dh = qx.shape[-1] // X_HEADS
    outs = []
    for hd in range(X_HEADS):
        sl = slice(hd * dh, (hd + 1) * dh)
        sc = _dot_nt(qx[:, sl].astype(BF16), kx_ref[0, :, sl]) * (dh ** -0.5)
        p = jnp.exp(sc - jnp.max(sc, axis=-1, keepdims=True))
        p = p / jnp.sum(p, axis=-1, keepdims=True)
        outs.append(_dot(p.astype(BF16), vx_ref[0, :, sl]))
    o = jnp.concatenate(outs, axis=-1)
    o_ref[0] = h1 + _dot(o.astype(BF16), wxo_ref[...])


def _mix(h, a_out, yt, ut, d, wglut, bglu, ag, sg, woa, wos, xg, wxq, kx, vx, wxo, *, ts):
    B, S, D = h.shape
    n_ssm = yt.shape[1]
    M = kx.shape[1]
    row = lambda b, i: (b, i, 0)
    colmaj = pl.BlockSpec((1, n_ssm, ts), lambda b, i: (b, 0, i))
    mem = pl.BlockSpec((1, M, D), lambda b, i: (b, 0, 0))
    consts = [d, wglut, bglu, ag, sg, woa, wos, xg, wxq]
    return pl.pallas_call(
        _mix_kernel, grid=(B, S // ts),
        in_specs=[pl.BlockSpec((1, ts, D), row), pl.BlockSpec((1, ts, a_out.shape[-1]), row), colmaj, colmaj]
                 + [_const_spec(c.shape) for c in consts] + [mem, mem, _const_spec(wxo.shape)],
        out_specs=pl.BlockSpec((1, ts, D), row),
        out_shape=jax.ShapeDtypeStruct((B, S, D), F32),
        compiler_params=_params("parallel", "parallel"), name="mix",
    )(h, a_out, yt, ut, *consts, kx, vx, wxo)


def _swiglu_kernel(h_ref, g_ref, wg_ref, wu_ref, wd_ref, fg_ref, o_ref, hn_ref, acc_ref, *, final_norm):
    j = pl.program_id(1)

    @pl.when(j == 0)
    def _():
        hn_ref[...] = _rms_rows(h_ref[...], g_ref[...]).astype(BF16)
        acc_ref[...] = jnp.zeros_like(acc_ref)

    hn = hn_ref[...]
    gate = _dot(hn, wg_ref[...])
    up = _dot(hn, wu_ref[...])
    acc_ref[...] += _dot((gate * jax.nn.sigmoid(gate) * up).astype(BF16), wd_ref[...])

    @pl.when(j == pl.num_programs(1) - 1)
    def _():
        out = h_ref[...] + acc_ref[...]
        if final_norm:
            out = _rms_rows(out, fg_ref[...])
        o_ref[...] = out


def _swiglu(h2d, g, wg, wu, wd, fg, *, tm, tf, final_norm):
    T, D = h2d.shape
    F = wg.shape[1]
    return pl.pallas_call(
        functools.partial(_swiglu_kernel, final_norm=final_norm), grid=(T // tm, F // tf),
        in_specs=[pl.BlockSpec((tm, D), lambda i, j: (i, 0)), _const_spec(g.shape),
                  pl.BlockSpec((D, tf), lambda i, j: (0, j)), pl.BlockSpec((D, tf), lambda i, j: (0, j)),
                  pl.BlockSpec((tf, D), lambda i, j: (j, 0)), _const_spec(fg.shape)],
        out_specs=pl.BlockSpec((tm, D), lambda i, j: (i, 0)),
        out_shape=jax.ShapeDtypeStruct((T, D), F32),
        scratch_shapes=[pltpu.VMEM((tm, D), BF16), pltpu.VMEM((tm, D), F32)],
        compiler_params=_params("parallel", "arbitrary"), name="swiglu",
    )(h2d, g, wg, wu, wd, fg)


def _rope_swap(w):
    half = w.shape[-1] // 2
    return jnp.concatenate([-w[..., half:], w[..., :half]], axis=-1)


def _tile_of(d, prefs):
    for t in prefs:
        if d % t == 0:
            return t
    return d


def kernel(x, mem, positions, norm_mix_g, w_in, q_norm_g, w_uq, kv_norm_g, w_ukv, ssm_lambda_re, ssm_lambda_im,
           ssm_log_dt, ssm_b_re, ssm_b_im, ssm_c_re, ssm_c_im, ssm_d, ssm_w_glu, ssm_b_glu, attn_out_g, ssm_out_g,
           w_out, norm_x_g, mem_norm_g, w_xq, w_xkv, w_xo, norm_ffn_g, w_gate, w_up, w_down, final_norm_g):
    B, S, D = x.shape
    depth = w_in.shape[0]
    n_ssm = ssm_d.shape[-1]
    n_groups = n_ssm // SSM_GROUP
    mla_w = HEADS * VDIM
    assert S % CHUNK == 0
    nk = S // CHUNK
    tm = _tile_of(S, (512, 256, 128))
    tff = _tile_of(w_gate.shape[-1], (1408, 256, 128))

    pos3 = positions.reshape(B, S, 1)
    freqs = ROPE_THETA ** (-jnp.arange(0, ROPE, 2, dtype=F32) / ROPE)
    freq128 = jnp.zeros((1, LANES), F32).at[0, NOPE:NOPE + ROPE].set(jnp.concatenate([freqs, freqs]))
    zeros = lambda *s: jnp.zeros(s, F32)

    h = x
    for l in range(depth):
        wi = w_in[l]
        w_kr = wi[:, Q_LORA + KV_LORA:Q_LORA + KV_LORA + ROPE]
        pad_rope = lambda w: jnp.concatenate([zeros(D, NOPE), w, zeros(D, HEAD_PAD - NOPE - ROPE)], axis=1)
        wcat = jnp.concatenate([wi[:, :Q_LORA + KV_LORA], pad_rope(w_kr), pad_rope(_rope_swap(w_kr))], axis=1).astype(BF16)
        wut = wi[:, Q_LORA + KV_LORA + ROPE:].T.astype(BF16)
        wq = w_uq[l].reshape(Q_LORA, HEADS, NOPE + ROPE)
        zq = zeros(Q_LORA, HEADS, HEAD_PAD - NOPE - ROPE)
        wqm = jnp.concatenate([wq, zq], axis=-1).reshape(Q_LORA, HEADS * HEAD_PAD).astype(BF16)
        wqs = jnp.concatenate([zeros(Q_LORA, HEADS, NOPE), _rope_swap(wq[..., NOPE:]), zq], axis=-1)
        wqs = wqs.reshape(Q_LORA, HEADS * HEAD_PAD).astype(BF16)
        wkv = w_ukv[l].reshape(KV_LORA, HEADS, NOPE + VDIM)
        wk = jnp.concatenate([wkv[..., :NOPE], zeros(KV_LORA, HEADS, HEAD_PAD - NOPE)], axis=-1)
        wk = wk.reshape(KV_LORA, HEADS * HEAD_PAD).astype(BF16)
        wv = wkv[..., NOPE:].reshape(KV_LORA, mla_w).astype(BF16)

        q, k, v, ut = _front(h, pos3, freq128, norm_mix_g[l][None], wcat, wut, q_norm_g[l][None], wqm, wqs,
                             kv_norm_g[l][None], wk, wv, tm=tm)
        a_out = _mla(q, k, v, tq=tm)

        lr = ssm_lambda_re[l]
        li = ssm_lambda_im[l]
        dup = lambda a: jnp.concatenate([a, a], axis=-1)
        brt = jnp.swapaxes(ssm_b_re[l], 1, 2)
        bit = jnp.swapaxes(ssm_b_im[l], 1, 2)
        crt = jnp.swapaxes(ssm_c_re[l], 1, 2)
        cit = jnp.swapaxes(ssm_c_im[l], 1, 2)
        yt4 = _s5(ut.reshape(B, n_groups, SSM_GROUP * nk, CHUNK), ssm_log_dt[l].reshape(n_groups, 1, 1),
                  dup(lr)[:, None, :], dup(li)[:, None, :], dup(lr)[:, :, None], dup(li)[:, :, None],
                  jnp.concatenate([brt, bit], axis=-1), jnp.concatenate([-bit, brt], axis=-1),
                  jnp.concatenate([crt, cit], axis=1), jnp.concatenate([-cit, crt], axis=1))
        yt = yt4.reshape(B, n_ssm, S)

        kx, vx = _memkv(mem, mem_norm_g[l][None], w_xkv[l].astype(BF16))
        h = _mix(h, a_out, yt, ut, ssm_d[l][:, None], ssm_w_glu[l].T.astype(BF16), ssm_b_glu[l][:, None],
                 attn_out_g[l][None], ssm_out_g[l][:, None], w_out[l][:mla_w].astype(BF16),
                 w_out[l][mla_w:].astype(BF16), norm_x_g[l][None], w_xq[l].astype(BF16), kx, vx,
                 w_xo[l].astype(BF16), ts=tm)
        h = _swiglu(h.reshape(B * S, D), norm_ffn_g[l][None], w_gate[l].astype(BF16), w_up[l].astype(BF16),
                    w_down[l].astype(BF16), final_norm_g[None], tm=tm, tf=tff,
                    final_norm=(l == depth - 1)).reshape(B, S, D)
    return h
```

```python
import functools
import math

import jax
import jax.numpy as jnp
from jax import lax
from jax.experimental import pallas as pl
from jax.experimental.pallas import tpu as pltpu

LANES = 128
MXU_WIDTH = 256
VMEM_LIMIT_BYTES = 56 * 1024 * 1024

EPS = 1e-6
HEADS = 8
NOPE = 64
ROPE = 32
VDIM = 64
HEAD_PAD = LANES
Q_LORA = 256
KV_LORA = 128
ROPE_THETA = 10000.0
SSM_GROUP = 16
SSM_STATE = 64
CHUNK = LANES
X_HEADS = 4

F32 = jnp.float32
BF16 = jnp.bfloat16
NEG = -1e30


def _dot(a, b):
    return jnp.dot(a, b, preferred_element_type=F32)


def _dot_nt(a, b):
    return lax.dot_general(a, b, (((1,), (1,)), ((), ())), preferred_element_type=F32)


def _rms_rows(x, g):
    return x * lax.rsqrt(jnp.mean(x * x, axis=-1, keepdims=True) + EPS) * g


def _params(*sem):
    return pltpu.CompilerParams(dimension_semantics=sem, vmem_limit_bytes=VMEM_LIMIT_BYTES)


def _const_spec(shape):
    nd = len(shape)
    return pl.BlockSpec(shape, lambda *_: (0,) * nd)


def _front_kernel(h_ref, pos_ref, freq_ref, gmix_ref, wcat_ref, wut_ref, qg_ref, wqm_ref, wqs_ref,
                  kvg_ref, wk_ref, wv_ref, q_ref, k_ref, v_ref, ut_ref):
    h = h_ref[0]
    xn = _rms_rows(h, gmix_ref[...]).astype(BF16)
    proj = _dot(xn, wcat_ref[...])
    cq = proj[:, :Q_LORA]
    ckv = proj[:, Q_LORA:Q_LORA + KV_LORA]
    kr_main = proj[:, Q_LORA + KV_LORA:Q_LORA + KV_LORA + LANES]
    kr_swap = proj[:, Q_LORA + KV_LORA + LANES:]
    ang = pos_ref[0].astype(F32) * freq_ref[...]
    cos = jnp.cos(ang)
    sin = jnp.sin(ang)
    scale = (NOPE + ROPE) ** -0.5 * math.log2(math.e)
    cqn = _rms_rows(cq, qg_ref[...]).astype(BF16)
    ckvn = _rms_rows(ckv, kvg_ref[...]).astype(BF16)
    qm = _dot(cqn, wqm_ref[...])
    qs = _dot(cqn, wqs_ref[...])
    cos_q = jnp.tile(cos * scale, (1, HEADS))
    sin_q = jnp.tile(sin * scale, (1, HEADS))
    q_ref[0] = (qm * cos_q + qs * sin_q).astype(BF16)
    k_rot = kr_main * cos + kr_swap * sin
    k_ref[0] = (_dot(ckvn, wk_ref[...]) + jnp.tile(k_rot, (1, HEADS))).astype(BF16)
    vlane = lax.broadcasted_iota(jnp.int32, (1, HEADS * HEAD_PAD), 1) % HEAD_PAD
    ones = jnp.where(vlane == VDIM, 1.0, 0.0).astype(F32)
    v_ref[0] = (_dot(ckvn, wv_ref[...]) + ones).astype(BF16)
    ut_ref[0] = _dot_nt(wut_ref[...], xn).astype(BF16)


def _front(h, pos3, freq, gmix, wcat, wut, qg, wqm, wqs, kvg, wk, wv, *, tm):
    B, S, D = h.shape
    n_ssm = wut.shape[0]
    grid = (B, S // tm)
    row = lambda b, i: (b, i, 0)
    in_specs = [
        pl.BlockSpec((1, tm, D), row),
        pl.BlockSpec((1, tm, 1), row),
        _const_spec(freq.shape), _const_spec(gmix.shape), _const_spec(wcat.shape), _const_spec(wut.shape),
        _const_spec(qg.shape), _const_spec(wqm.shape), _const_spec(wqs.shape),
        _const_spec(kvg.shape), _const_spec(wk.shape), _const_spec(wv.shape),
    ]
    out_specs = [
        pl.BlockSpec((1, tm, HEADS * HEAD_PAD), row),
        pl.BlockSpec((1, tm, HEADS * HEAD_PAD), row),
        pl.BlockSpec((1, tm, HEADS * HEAD_PAD), row),
        pl.BlockSpec((1, n_ssm, tm), lambda b, i: (b, 0, i)),
    ]
    out_shape = [
        jax.ShapeDtypeStruct((B, S, HEADS * HEAD_PAD), BF16),
        jax.ShapeDtypeStruct((B, S, HEADS * HEAD_PAD), BF16),
        jax.ShapeDtypeStruct((B, S, HEADS * HEAD_PAD), BF16),
        jax.ShapeDtypeStruct((B, n_ssm, S), BF16),
    ]
    return pl.pallas_call(
        _front_kernel, grid=grid, in_specs=in_specs, out_specs=out_specs, out_shape=out_shape,
        compiler_params=_params("parallel", "parallel"), name="front",
    )(h, pos3, freq, gmix, wcat, wut, qg, wqm, wqs, kvg, wk, wv)


def _mla_kernel(q_ref, k_ref, v_ref, o_ref, *, tq):
    qi = pl.program_id(2)
    row = lax.broadcasted_iota(jnp.int32, (tq, tq), 0)
    col = lax.broadcasted_iota(jnp.int32, (tq, tq), 1)
    causal = col <= row
    qs = [q_ref[0, :, hh * HEAD_PAD:(hh + 1) * HEAD_PAD] for hh in range(2)]

    def step(j, carry, masked):
        start = pl.multiple_of(j * tq, tq)
        out = []
        for hh in range(2):
            m, acc = carry[hh]
            kc = k_ref[0, pl.ds(start, tq), hh * HEAD_PAD:(hh + 1) * HEAD_PAD]
            vc = v_ref[0, pl.ds(start, tq), hh * HEAD_PAD:(hh + 1) * HEAD_PAD]
            s = _dot_nt(qs[hh], kc)
            if masked:
                s = jnp.where(causal, s, NEG)
            m_new = jnp.maximum(m, jnp.max(s, axis=-1, keepdims=True))
            p = jnp.exp2(s - m_new)
            acc = jnp.exp2(m - m_new) * acc + _dot(p.astype(BF16), vc)
            out.append((m_new, acc))
        return tuple(out)

    def two_steps(j, carry):
        return step(2 * j + 1, step(2 * j, carry, False), False)

    init = (jnp.full((tq, 1), NEG, F32), jnp.zeros((tq, HEAD_PAD), F32))
    carry = lax.fori_loop(0, qi // 2, two_steps, (init, init))
    carry = lax.cond(qi % 2 == 1,
                     lambda c: step(qi, step(qi - 1, c, False), True),
                     lambda c: step(qi, c, True), carry)
    o = [acc / acc[:, VDIM:VDIM + 1] for _, acc in carry]
    lane = lax.broadcasted_iota(jnp.int32, (tq, HEAD_PAD), 1)
    o_ref[0] = jnp.where(lane < VDIM, o[0], pltpu.roll(o[1], VDIM, 1)).astype(o_ref.dtype)


def _mla(q, k, v, *, tq):
    B, S, _ = q.shape
    grid = (B, HEADS // 2, S // tq)
    return pl.pallas_call(
        functools.partial(_mla_kernel, tq=tq), grid=grid,
        in_specs=[
            pl.BlockSpec((1, tq, 2 * HEAD_PAD), lambda b, p, i: (b, i, p)),
            pl.BlockSpec((1, S, 2 * HEAD_PAD), lambda b, p, i: (b, 0, p)),
            pl.BlockSpec((1, S, 2 * HEAD_PAD), lambda b, p, i: (b, 0, p)),
        ],
        out_specs=pl.BlockSpec((1, tq, 2 * VDIM), lambda b, p, i: (b, i, p)),
        out_shape=jax.ShapeDtypeStruct((B, S, HEADS * VDIM), BF16),
        compiler_params=_params("parallel", "parallel", "arbitrary"), name="mla",
    )(q, k, v)


def _cpow(zr, zi, e):
    mag = jnp.exp(e * zr)
    return mag * jnp.cos(e * zi), mag * jnp.sin(e * zi)


def _s5_kernel(u_ref, ldt_ref, lrr_ref, lir_ref, lrc_ref, lic_ref, bx_ref, by_ref, cx_ref, cy_ref,
               y_ref, tt_ref, wst_ref, wout_ref, ep_ref, km_ref, lhs_ref, *, nb, nk):
    P2 = 2 * SSM_STATE
    G = SSM_GROUP
    dt = jnp.exp(ldt_ref[0])
    lr = lrr_ref[0]
    li = lir_ref[0]
    zr = lr * dt
    zi = li * dt
    ar, ai = _cpow(zr, zi, 1.0)
    den = lr * lr + li * li
    cr = ((ar - 1.0) * lr + ai * li) / den
    ci = (ai * lr - (ar - 1.0) * li) / den
    bx = bx_ref[0]
    by = by_ref[0]
    bbx = cr * bx + ci * by
    bby = cr * by - ci * bx
    lane = lax.broadcasted_iota(jnp.int32, (1, P2), 1)
    sign_lane = jnp.where(lane < SSM_STATE, 1.0, -1.0).astype(F32)
    e_rows = (CHUNK - 1 - lax.broadcasted_iota(jnp.int32, (CHUNK, 1), 0)).astype(F32)
    pr, pi = _cpow(zr, zi, e_rows)
    for c in range(G):
        wst_ref[c * CHUNK:(c + 1) * CHUNK, :] = (pr * bbx[c:c + 1, :] + pi * bby[c:c + 1, :]).astype(BF16)
    zrc = lrc_ref[0] * dt
    zic = lic_ref[0] * dt
    tau = lax.broadcasted_iota(jnp.int32, (1, CHUNK), 1).astype(F32)
    a0r, a0i = _cpow(zrc, zic, tau)
    a1r, a1i = _cpow(zrc, zic, tau + 1.0)
    cx = cx_ref[0]
    cy = cy_ref[0]
    sub = lax.broadcasted_iota(jnp.int32, (P2, 1), 0)
    sign_sub = jnp.where(sub < SSM_STATE, 1.0, -1.0).astype(F32)
    for c in range(G):
        cxc = cx[:, c:c + 1]
        cyc = cy[:, c:c + 1]
        ep_ref[:, c * CHUNK:(c + 1) * CHUNK] = a0r * cxc + a0i * cyc
        wout_ref[:, c * CHUNK:(c + 1) * CHUNK] = ((a1r * cxc + a1i * cyc) * sign_sub).astype(BF16)
    km_ref[...] = jnp.dot(bbx * sign_lane, ep_ref[...], preferred_element_type=F32,
                          precision=lax.Precision.HIGHEST)
    jj = lax.broadcasted_iota(jnp.int32, (CHUNK, CHUNK), 0)
    tcol = lax.broadcasted_iota(jnp.int32, (CHUNK, CHUNK), 1)
    lower = tcol >= jj

    def toeplitz_rows(cp, _):
        krow = km_ref[pl.ds(cp, 1), :]
        r0 = pl.multiple_of(cp * CHUNK, CHUNK)
        for c in range(G):
            blk = jnp.broadcast_to(krow[:, c * CHUNK:(c + 1) * CHUNK], (CHUNK, CHUNK))
            blk = pltpu.roll(blk, 0, 1, stride=1, stride_axis=0)
            tt_ref[pl.ds(r0, CHUNK), c * CHUNK:(c + 1) * CHUNK] = jnp.where(lower, blk, 0.0).astype(BF16)
        return 0

    lax.fori_loop(0, G, toeplitz_rows, 0)
    for b in range(nb):
        for c in range(G):
            lhs_ref[b * nk:(b + 1) * nk, c * CHUNK:(c + 1) * CHUNK] = u_ref[b, 0, c * nk:(c + 1) * nk, :]
    lhs = lhs_ref[...]
    y = _dot(lhs, tt_ref[...])
    x = _dot(lhs, wst_ref[...])
    kidx = lax.broadcasted_iota(jnp.int32, (nb * nk, 1), 0) % nk
    qr, qi = ar, ai
    for _ in range(int(math.log2(CHUNK))):
        qr, qi = qr * qr - qi * qi, 2.0 * qr * qi
    d = 1
    while d < nk:
        sx = jnp.where(kidx >= d, pltpu.roll(x, d, 0), 0.0)
        x = x + qr * sx + (qi * -sign_lane) * pltpu.roll(sx, SSM_STATE, 1)
        qr, qi = qr * qr - qi * qi, 2.0 * qr * qi
        d *= 2
    x0 = jnp.where(kidx >= 1, pltpu.roll(x, 1, 0), 0.0)
    y = y + _dot(x0.astype(BF16), wout_ref[...])
    for b in range(nb):
        for c in range(G):
            y_ref[b, 0, c * nk:(c + 1) * nk, :] = y[b * nk:(b + 1) * nk, c * CHUNK:(c + 1) * CHUNK]


def _s5(u4, ldt, lrr, lir, lrc, lic, bx, by, cx, cy):
    B, NG, R, _ = u4.shape
    nk = R // SSM_GROUP
    gspec = lambda shape: pl.BlockSpec((1,) + shape, lambda g: (g, 0, 0))
    ublk = pl.BlockSpec((B, 1, R, CHUNK), lambda g: (0, g, 0, 0))
    W = SSM_GROUP * CHUNK
    return pl.pallas_call(
        functools.partial(_s5_kernel, nb=B, nk=nk), grid=(NG,),
        in_specs=[ublk, gspec((1, 1)), gspec((1, 2 * SSM_STATE)), gspec((1, 2 * SSM_STATE)),
                  gspec((2 * SSM_STATE, 1)), gspec((2 * SSM_STATE, 1)),
                  gspec((SSM_GROUP, 2 * SSM_STATE)), gspec((SSM_GROUP, 2 * SSM_STATE)),
                  gspec((2 * SSM_STATE, SSM_GROUP)), gspec((2 * SSM_STATE, SSM_GROUP))],
        out_specs=ublk,
        out_shape=jax.ShapeDtypeStruct(u4.shape, F32),
        scratch_shapes=[
            pltpu.VMEM((W, W), BF16),
            pltpu.VMEM((W, 2 * SSM_STATE), BF16),
            pltpu.VMEM((2 * SSM_STATE, W), BF16),
            pltpu.VMEM((2 * SSM_STATE, W), F32),
            pltpu.VMEM((SSM_GROUP, W), F32),
            pltpu.VMEM((B * nk, W), BF16),
        ],
        compiler_params=_params("parallel"), name="s5",
    )(u4, ldt, lrr, lir, lrc, lic, bx, by, cx, cy)


def _memkv_kernel(mem_ref, g_ref, w_ref, k_ref, v_ref):
    mn = _rms_rows(mem_ref[0], g_ref[...]).astype(BF16)
    kv = _dot(mn, w_ref[...])
    d = k_ref.shape[-1]
    k_ref[0] = kv[:, :d].astype(BF16)
    v_ref[0] = kv[:, d:].astype(BF16)


def _memkv(mem, g, w):
    B, M, D = mem.shape
    blk = pl.BlockSpec((1, M, D), lambda b: (b, 0, 0))
    return pl.pallas_call(
        _memkv_kernel, grid=(B,),
        in_specs=[blk, _const_spec(g.shape), _const_spec(w.shape)],
        out_specs=[blk, blk],
        out_shape=[jax.ShapeDtypeStruct((B, M, D), BF16)] * 2,
        compiler_params=_params("parallel"), name="memkv",
    )(mem, g, w)


def _mix_kernel(h_ref, a_ref, yt_ref, ut_ref, d_ref, wglut_ref, bglu_ref, ag_ref, sg_ref, woa_ref, wos_ref,
                xg_ref, wxq_ref, kx_ref, vx_ref, wxo_ref, o_ref):
    y = yt_ref[0] + d_ref[...] * ut_ref[0].astype(F32)
    g = jax.nn.gelu(y)
    z = _dot(wglut_ref[...], g.astype(BF16)) + bglu_ref[...]
    s = y * jax.nn.sigmoid(z)
    s_n = s * lax.rsqrt(jnp.mean(s * s, axis=0, keepdims=True) + EPS) * sg_ref[...]
    a_n = _rms_rows(a_ref[0].astype(F32), ag_ref[...])
    h1 = h_ref[0] + _dot(a_n.astype(BF16), woa_ref[...]) + _dot(s_n.T.astype(BF16), wos_ref[...])
    qx = _dot(_rms_rows(h1, xg_ref[...]).astype(BF16), wxq_ref[...])
    dh = qx.shape[-1] // X_HEADS
    outs = []
    for hd in range(X_HEADS):
        sl = slice(hd * dh, (hd + 1) * dh)
        sc = _dot_nt(qx[:, sl].astype(BF16), kx_ref[0, :, sl]) * (dh ** -0.5)
        p = jnp.exp(sc - jnp.max(sc, axis=-1, keepdims=True))
        p = p / jnp.sum(p, axis=-1, keepdims=True)
        outs.append(_dot(p.astype(BF16), vx_ref[0, :, sl]))
    o = jnp.concatenate(outs, axis=-1)
    o_ref[0] = h1 + _dot(o.astype(BF16), wxo_ref[...])


def _mix(h, a_out, yt, ut, d, wglut, bglu, ag, sg, woa, wos, xg, wxq, kx, vx, wxo, *, ts):
    B, S, D = h.shape
    n_ssm = yt.shape[1]
    M = kx.shape[1]
    row = lambda b, i: (b, i, 0)
    colmaj = pl.BlockSpec((1, n_ssm, ts), lambda b, i: (b, 0, i))
    mem = pl.BlockSpec((1, M, D), lambda b, i: (b, 0, 0))
    consts = [d, wglut, bglu, ag, sg, woa, wos, xg, wxq]
    return pl.pallas_call(
        _mix_kernel, grid=(B, S // ts),
        in_specs=[pl.BlockSpec((1, ts, D), row), pl.BlockSpec((1, ts, a_out.shape[-1]), row), colmaj, colmaj]
                 + [_const_spec(c.shape) for c in consts] + [mem, mem, _const_spec(wxo.shape)],
        out_specs=pl.BlockSpec((1, ts, D), row),
        out_shape=jax.ShapeDtypeStruct((B, S, D), F32),
        compiler_params=_params("parallel", "parallel"), name="mix",
    )(h, a_out, yt, ut, *consts, kx, vx, wxo)


def _swiglu_kernel(h_ref, g_ref, wg_ref, wu_ref, wd_ref, fg_ref, o_ref, *, final_norm, chunks):
    h = h_ref[...]
    hn = _rms_rows(h, g_ref[...]).astype(BF16)
    out = h
    lo = 0
    for width in chunks:
        gate = _dot(hn, wg_ref[:, lo:lo + width])
        up = _dot(hn, wu_ref[:, lo:lo + width])
        out = out + _dot((gate * jax.nn.sigmoid(gate) * up).astype(BF16), wd_ref[lo:lo + width, :])
        lo += width
    if final_norm:
        out = _rms_rows(out, fg_ref[...])
    o_ref[...] = out


def _resident_spec(shape):
    nd = len(shape)
    return pl.BlockSpec(shape, lambda *_: (0,) * nd, pipeline_mode=pl.Buffered(1))


def _swiglu(h2d, g, wg, wu, wd, fg, *, tm, chunks, final_norm):
    T, D = h2d.shape
    return pl.pallas_call(
        functools.partial(_swiglu_kernel, final_norm=final_norm, chunks=chunks), grid=(T // tm,),
        in_specs=[pl.BlockSpec((tm, D), lambda i: (i, 0)), _const_spec(g.shape),
                  _resident_spec(wg.shape), _resident_spec(wu.shape), _resident_spec(wd.shape),
                  _const_spec(fg.shape)],
        out_specs=pl.BlockSpec((tm, D), lambda i: (i, 0)),
        out_shape=jax.ShapeDtypeStruct((T, D), F32),
        compiler_params=_params("parallel"), name="swiglu",
    )(h2d, g, wg, wu, wd, fg)


def _rope_swap(w):
    half = w.shape[-1] // 2
    return jnp.concatenate([-w[..., half:], w[..., :half]], axis=-1)


def _tile_of(d, prefs):
    for t in prefs:
        if d % t == 0:
            return t
    return d


def _split(total, target, align):
    n = -(-total // target)
    units = -(-total // align)
    widths = [(units // n + (1 if i < units % n else 0)) * align for i in range(n)]
    widths[-1] -= sum(widths) - total
    return tuple(w for w in widths if w > 0)


def kernel(x, mem, positions, norm_mix_g, w_in, q_norm_g, w_uq, kv_norm_g, w_ukv, ssm_lambda_re, ssm_lambda_im,
           ssm_log_dt, ssm_b_re, ssm_b_im, ssm_c_re, ssm_c_im, ssm_d, ssm_w_glu, ssm_b_glu, attn_out_g, ssm_out_g,
           w_out, norm_x_g, mem_norm_g, w_xq, w_xkv, w_xo, norm_ffn_g, w_gate, w_up, w_down, final_norm_g):
    B, S, D = x.shape
    depth = w_in.shape[0]
    n_ssm = ssm_d.shape[-1]
    n_groups = n_ssm // SSM_GROUP
    mla_w = HEADS * VDIM
    assert S % CHUNK == 0
    nk = S // CHUNK
    tm = _tile_of(S, (512, 256, 128))
    ff_chunks = _split(w_gate.shape[-1], 4 * MXU_WIDTH, MXU_WIDTH)

    pos3 = positions.reshape(B, S, 1)
    freqs = ROPE_THETA ** (-jnp.arange(0, ROPE, 2, dtype=F32) / ROPE)
    freq128 = jnp.zeros((1, LANES), F32).at[0, NOPE:NOPE + ROPE].set(jnp.concatenate([freqs, freqs]))
    zeros = lambda *s: jnp.zeros(s, F32)

    h = x
    for l in range(depth):
        wi = w_in[l]
        w_kr = wi[:, Q_LORA + KV_LORA:Q_LORA + KV_LORA + ROPE]
        pad_rope = lambda w: jnp.concatenate([zeros(D, NOPE), w, zeros(D, HEAD_PAD - NOPE - ROPE)], axis=1)
        wcat = jnp.concatenate([wi[:, :Q_LORA + KV_LORA], pad_rope(w_kr), pad_rope(_rope_swap(w_kr))], axis=1).astype(BF16)
        wut = wi[:, Q_LORA + KV_LORA + ROPE:].T.astype(BF16)
        wq = w_uq[l].reshape(Q_LORA, HEADS, NOPE + ROPE)
        zq = zeros(Q_LORA, HEADS, HEAD_PAD - NOPE - ROPE)
        wqm = jnp.concatenate([wq, zq], axis=-1).reshape(Q_LORA, HEADS * HEAD_PAD).astype(BF16)
        wqs = jnp.concatenate([zeros(Q_LORA, HEADS, NOPE), _rope_swap(wq[..., NOPE:]), zq], axis=-1)
        wqs = wqs.reshape(Q_LORA, HEADS * HEAD_PAD).astype(BF16)
        wkv = w_ukv[l].reshape(KV_LORA, HEADS, NOPE + VDIM)
        wk = jnp.concatenate([wkv[..., :NOPE], zeros(KV_LORA, HEADS, HEAD_PAD - NOPE)], axis=-1)
        wk = wk.reshape(KV_LORA, HEADS * HEAD_PAD).astype(BF16)
        wv = jnp.concatenate([wkv[..., NOPE:], zeros(KV_LORA, HEADS, HEAD_PAD - VDIM)], axis=-1)
        wv = wv.reshape(KV_LORA, HEADS * HEAD_PAD).astype(BF16)

        q, k, v, ut = _front(h, pos3, freq128, norm_mix_g[l][None], wcat, wut, q_norm_g[l][None], wqm, wqs,
                             kv_norm_g[l][None], wk, wv, tm=tm)
        a_out = _mla(q, k, v, tq=tm)

        lr = ssm_lambda_re[l]
        li = ssm_lambda_im[l]
        dup = lambda a: jnp.concatenate([a, a], axis=-1)
        brt = jnp.swapaxes(ssm_b_re[l], 1, 2)
        bit = jnp.swapaxes(ssm_b_im[l], 1, 2)
        crt = jnp.swapaxes(ssm_c_re[l], 1, 2)
        cit = jnp.swapaxes(ssm_c_im[l], 1, 2)
        yt4 = _s5(ut.reshape(B, n_groups, SSM_GROUP * nk, CHUNK), ssm_log_dt[l].reshape(n_groups, 1, 1),
                  dup(lr)[:, None, :], dup(li)[:, None, :], dup(lr)[:, :, None], dup(li)[:, :, None],
                  jnp.concatenate([brt, bit], axis=-1), jnp.concatenate([-bit, brt], axis=-1),
                  jnp.concatenate([crt, cit], axis=1), jnp.concatenate([-cit, crt], axis=1))
        yt = yt4.reshape(B, n_ssm, S)

        kx, vx = _memkv(mem, mem_norm_g[l][None], w_xkv[l].astype(BF16))
        h = _mix(h, a_out, yt, ut, ssm_d[l][:, None], ssm_w_glu[l].T.astype(BF16), ssm_b_glu[l][:, None],
                 attn_out_g[l][None], ssm_out_g[l][:, None], w_out[l][:mla_w].astype(BF16),
                 w_out[l][mla_w:].astype(BF16), norm_x_g[l][None], w_xq[l].astype(BF16), kx, vx,
                 w_xo[l].astype(BF16), ts=tm)
        h = _swiglu(h.reshape(B * S, D), norm_ffn_g[l][None], w_gate[l].astype(BF16), w_up[l].astype(BF16),
                    w_down[l].astype(BF16), final_norm_g[None], tm=tm, chunks=ff_chunks,
                    final_norm=(l == depth - 1)).reshape(B, S, D)
    return h
```

```python
import functools
import math

import jax
import jax.numpy as jnp
from jax import lax
from jax.experimental import pallas as pl
from jax.experimental.pallas import tpu as pltpu

LANES = 128
MXU_WIDTH = 256
VMEM_LIMIT_BYTES = 56 * 1024 * 1024

EPS = 1e-6
HEADS = 8
NOPE = 64
ROPE = 32
VDIM = 64
HEAD_PAD = LANES
V_SUM_LANE = (VDIM, 0)
Q_LORA = 256
KV_LORA = 128
ROPE_THETA = 10000.0
SSM_GROUP = 16
SSM_STATE = 64
CHUNK = LANES
X_HEADS = 4
MLA_TK = 512
MLA_UNROLL = 7

F32 = jnp.float32
BF16 = jnp.bfloat16
NEG = -1e30


def _dot(a, b):
    return jnp.dot(a, b, preferred_element_type=F32)


def _dot_nt(a, b):
    return lax.dot_general(a, b, (((1,), (1,)), ((), ())), preferred_element_type=F32)


def _rms_rows(x, g):
    return x * lax.rsqrt(jnp.mean(x * x, axis=-1, keepdims=True) + EPS) * g


def _params(*sem):
    return pltpu.CompilerParams(dimension_semantics=sem, vmem_limit_bytes=VMEM_LIMIT_BYTES)


def _const_spec(shape):
    nd = len(shape)
    return pl.BlockSpec(shape, lambda *_: (0,) * nd)


def _front_kernel(h_ref, pos_ref, freq_ref, gmix_ref, wcat_ref, wut_ref, qg_ref, wqm_ref, wqs_ref,
                  kvg_ref, wk_ref, wv_ref, q_ref, k_ref, v_ref, ut_ref):
    h = h_ref[0]
    xn = _rms_rows(h, gmix_ref[...]).astype(BF16)
    proj = _dot(xn, wcat_ref[...])
    cq = proj[:, :Q_LORA]
    ckv = proj[:, Q_LORA:Q_LORA + KV_LORA]
    kr_main = proj[:, Q_LORA + KV_LORA:Q_LORA + KV_LORA + LANES]
    kr_swap = proj[:, Q_LORA + KV_LORA + LANES:]
    ang = pos_ref[0].astype(F32) * freq_ref[...]
    cos = jnp.cos(ang)
    sin = jnp.sin(ang)
    scale = (NOPE + ROPE) ** -0.5 * math.log2(math.e)
    cqn = _rms_rows(cq, qg_ref[...]).astype(BF16)
    ckvn = _rms_rows(ckv, kvg_ref[...]).astype(BF16)
    qm = _dot(cqn, wqm_ref[...])
    qs = _dot(cqn, wqs_ref[...])
    cos_q = jnp.tile(cos * scale, (1, HEADS))
    sin_q = jnp.tile(sin * scale, (1, HEADS))
    q_ref[0] = (qm * cos_q + qs * sin_q).astype(BF16)
    k_rot = kr_main * cos + kr_swap * sin
    k_ref[0] = (_dot(ckvn, wk_ref[...]) + jnp.tile(k_rot, (1, HEADS))).astype(BF16)
    vlane = lax.broadcasted_iota(jnp.int32, (1, HEADS * HEAD_PAD), 1) % (2 * HEAD_PAD)
    ones = jnp.where((vlane == V_SUM_LANE[0]) | (vlane == HEAD_PAD + V_SUM_LANE[1]), 1.0, 0.0).astype(F32)
    v_ref[0] = (_dot(ckvn, wv_ref[...]) + ones).astype(BF16)
    ut = _dot_nt(wut_ref[...], xn)
    n_groups = ut_ref.shape[1]
    for dk in range(ut.shape[1] // CHUNK):
        ut_ref[0, :, dk * SSM_GROUP:(dk + 1) * SSM_GROUP, :] = (
            ut[:, dk * CHUNK:(dk + 1) * CHUNK].reshape(n_groups, SSM_GROUP, CHUNK))


def _front(h, pos3, freq, gmix, wcat, wut, qg, wqm, wqs, kvg, wk, wv, *, tm):
    B, S, D = h.shape
    n_ssm = wut.shape[0]
    grid = (B, S // tm)
    row = lambda b, i: (b, i, 0)
    in_specs = [
        pl.BlockSpec((1, tm, D), row),
        pl.BlockSpec((1, tm, 1), row),
        _const_spec(freq.shape), _const_spec(gmix.shape), _const_spec(wcat.shape), _const_spec(wut.shape),
        _const_spec(qg.shape), _const_spec(wqm.shape), _const_spec(wqs.shape),
        _const_spec(kvg.shape), _const_spec(wk.shape), _const_spec(wv.shape),
    ]
    out_specs = [
        pl.BlockSpec((1, tm, HEADS * HEAD_PAD), row),
        pl.BlockSpec((1, tm, HEADS * HEAD_PAD), row),
        pl.BlockSpec((1, tm, HEADS * HEAD_PAD), row),
        pl.BlockSpec((1, n_ssm // SSM_GROUP, tm // CHUNK * SSM_GROUP, CHUNK), lambda b, i: (b, 0, i, 0)),
    ]
    out_shape = [
        jax.ShapeDtypeStruct((B, S, HEADS * HEAD_PAD), BF16),
        jax.ShapeDtypeStruct((B, S, HEADS * HEAD_PAD), BF16),
        jax.ShapeDtypeStruct((B, S, HEADS * HEAD_PAD), BF16),
        jax.ShapeDtypeStruct((B, n_ssm // SSM_GROUP, S // CHUNK * SSM_GROUP, CHUNK), F32),
    ]
    return pl.pallas_call(
        _front_kernel, grid=grid, in_specs=in_specs, out_specs=out_specs, out_shape=out_shape,
        compiler_params=_params("parallel", "parallel"), name="front",
    )(h, pos3, freq, gmix, wcat, wut, qg, wqm, wqs, kvg, wk, wv)


def _mla_kernel(qi_tab, kj_tab, bias_tab, q_ref, k_ref, v_ref, o_ref, bias_ref, *, tq, tk, n_steps, unroll):
    n = n_steps
    row = lax.broadcasted_iota(jnp.int32, (tq, tk), 0)
    col = lax.broadcasted_iota(jnp.int32, (tq, tk), 1)
    bias_ref[0] = jnp.zeros((tq, tk), F32)
    for d in range(tq // tk):
        bias_ref[1 + d] = jnp.where(col + d * tk <= row, 0.0, NEG).astype(F32)
    lane = lax.broadcasted_iota(jnp.int32, (tq, HEAD_PAD), 1)

    def rows(ref, r, size, hh):
        return ref[0, pl.ds(pl.multiple_of(r * size, size), size), hh * HEAD_PAD:(hh + 1) * HEAD_PAD]

    def scores(t):
        bias = bias_ref[bias_tab[t]]
        return [_dot_nt(rows(q_ref, qi_tab[t], tq, hh), rows(k_ref, kj_tab[t], tk, hh)) + bias for hh in range(2)]

    def finish(t, accs, ps):
        accs = [accs[hh] + _dot(ps[hh], rows(v_ref, kj_tab[t], tk, hh)) for hh in range(2)]
        o = [accs[hh] / accs[hh][:, V_SUM_LANE[hh]:V_SUM_LANE[hh] + 1] for hh in range(2)]
        start = pl.multiple_of(qi_tab[t] * tq, tq)
        o_ref[0, pl.ds(start, tq), :] = jnp.where(lane < VDIM, o[0], o[1]).astype(o_ref.dtype)
        return accs

    def softmax(t, ss, ms, accs):
        fresh = kj_tab[t] == 0
        new_ms, new_accs, new_ps = [], [], []
        for hh in range(2):
            m_in = jnp.where(fresh, NEG, ms[hh])
            m_new = jnp.maximum(m_in, jnp.max(ss[hh], axis=-1, keepdims=True))
            new_ps.append(jnp.exp2(ss[hh] - m_new).astype(BF16))
            new_ms.append(m_new)
            new_accs.append(jnp.exp2(m_in - m_new) * accs[hh])
        return new_ms, new_accs, new_ps

    def trip(t, state):
        ms, accs, ps = state
        ss = scores(t)
        accs = finish(t - 1, accs, ps)
        return softmax(t, ss, ms, accs)

    def trips(i, state):
        for u in range(unroll):
            state = trip(1 + i * unroll + u, state)
        return state

    zero = jnp.zeros((tq, HEAD_PAD), F32)
    neg = jnp.full((tq, 1), NEG, F32)
    state = softmax(0, scores(0), [neg, neg], [zero, zero])
    looped = (n - 1) // unroll
    state = lax.fori_loop(0, looped, trips, state)
    for t in range(1 + looped * unroll, n):
        state = trip(t, state)
    finish(n - 1, state[1], state[2])


def _mla(q, k, v, *, tq, tk, unroll):
    B, S, _ = q.shape
    per = tq // tk
    steps = [(i, j) for i in range(S // tq) for j in range((i + 1) * per)]
    tab = lambda f: jnp.asarray([f(i, j) for i, j in steps], jnp.int32)
    qi_tab, kj_tab = tab(lambda i, j: i), tab(lambda i, j: j)
    bias_tab = tab(lambda i, j: max(0, j - i * per + 1))
    pair = lambda width: pl.BlockSpec((1, S, width), lambda b, p, *_: (b, 0, p))
    grid_spec = pltpu.PrefetchScalarGridSpec(
        num_scalar_prefetch=3, grid=(B, HEADS // 2),
        in_specs=[pair(2 * HEAD_PAD), pair(2 * HEAD_PAD), pair(2 * HEAD_PAD)],
        out_specs=pair(2 * VDIM),
        scratch_shapes=[pltpu.VMEM((1 + per, tq, tk), F32)])
    return pl.pallas_call(
        functools.partial(_mla_kernel, tq=tq, tk=tk, n_steps=len(steps), unroll=unroll), grid_spec=grid_spec,
        out_shape=jax.ShapeDtypeStruct((B, S, HEADS * VDIM), BF16),
        compiler_params=_params("parallel", "parallel"), name="mla",
    )(qi_tab, kj_tab, bias_tab, q, k, v)


def _cpow(zr, zi, e):
    mag = jnp.exp(e * zr)
    return mag * jnp.cos(e * zi), mag * jnp.sin(e * zi)


def _s5_kernel(u_ref, ldt_ref, lrr_ref, lir_ref, lrc_ref, lic_ref, bx_ref, by_ref, cx_ref, cy_ref,
               y_ref, tt_ref, wst_ref, wout_ref, ep_ref, km_ref, lhs_ref, *, nb, nk):
    P2 = 2 * SSM_STATE
    G = SSM_GROUP
    dt = jnp.exp(ldt_ref[0])
    lr = lrr_ref[0]
    li = lir_ref[0]
    zr = lr * dt
    zi = li * dt
    ar, ai = _cpow(zr, zi, 1.0)
    den = lr * lr + li * li
    cr = ((ar - 1.0) * lr + ai * li) / den
    ci = (ai * lr - (ar - 1.0) * li) / den
    bx = bx_ref[0]
    by = by_ref[0]
    bbx = cr * bx + ci * by
    bby = cr * by - ci * bx
    lane = lax.broadcasted_iota(jnp.int32, (1, P2), 1)
    sign_lane = jnp.where(lane < SSM_STATE, 1.0, -1.0).astype(F32)
    e_rows = (CHUNK - 1 - lax.broadcasted_iota(jnp.int32, (CHUNK, 1), 0)).astype(F32)
    pr, pi = _cpow(zr, zi, e_rows)
    for c in range(G):
        wst_ref[c * CHUNK:(c + 1) * CHUNK, :] = (pr * bbx[c:c + 1, :] + pi * bby[c:c + 1, :]).astype(BF16)
    zrc = lrc_ref[0] * dt
    zic = lic_ref[0] * dt
    tau = lax.broadcasted_iota(jnp.int32, (1, CHUNK), 1).astype(F32)
    a0r, a0i = _cpow(zrc, zic, tau)
    a1r, a1i = _cpow(zrc, zic, tau + 1.0)
    cx = cx_ref[0]
    cy = cy_ref[0]
    sub = lax.broadcasted_iota(jnp.int32, (P2, 1), 0)
    sign_sub = jnp.where(sub < SSM_STATE, 1.0, -1.0).astype(F32)
    for c in range(G):
        cxc = cx[:, c:c + 1]
        cyc = cy[:, c:c + 1]
        ep_ref[:, c * CHUNK:(c + 1) * CHUNK] = a0r * cxc + a0i * cyc
        wout_ref[:, c * CHUNK:(c + 1) * CHUNK] = ((a1r * cxc + a1i * cyc) * sign_sub).astype(BF16)
    km_ref[...] = jnp.dot(bbx * sign_lane, ep_ref[...], preferred_element_type=F32,
                          precision=lax.Precision.HIGHEST)
    jj = lax.broadcasted_iota(jnp.int32, (CHUNK, CHUNK), 0)
    tcol = lax.broadcasted_iota(jnp.int32, (CHUNK, CHUNK), 1)
    lower = tcol >= jj

    def toeplitz_rows(cp, _):
        krow = km_ref[pl.ds(cp, 1), :]
        r0 = pl.multiple_of(cp * CHUNK, CHUNK)
        for c in range(G):
            blk = jnp.broadcast_to(krow[:, c * CHUNK:(c + 1) * CHUNK], (CHUNK, CHUNK))
            blk = pltpu.roll(blk, 0, 1, stride=1, stride_axis=0)
            tt_ref[pl.ds(r0, CHUNK), c * CHUNK:(c + 1) * CHUNK] = jnp.where(lower, blk, 0.0).astype(BF16)
        return 0

    lax.fori_loop(0, G, toeplitz_rows, 0)
    for b in range(nb):
        for c in range(G):
            lhs_ref[b * nk:(b + 1) * nk, c * CHUNK:(c + 1) * CHUNK] = (
                u_ref[b, 0, pl.ds(c, nk, stride=G), :].astype(BF16))
    lhs = lhs_ref[...]
    y = _dot(lhs, tt_ref[...])
    x = _dot(lhs, wst_ref[...])
    kidx = lax.broadcasted_iota(jnp.int32, (nb * nk, 1), 0) % nk
    qr, qi = ar, ai
    for _ in range(int(math.log2(CHUNK))):
        qr, qi = qr * qr - qi * qi, 2.0 * qr * qi
    d = 1
    while d < nk:
        sx = jnp.where(kidx >= d, pltpu.roll(x, d, 0), 0.0)
        x = x + qr * sx + (qi * -sign_lane) * pltpu.roll(sx, SSM_STATE, 1)
        qr, qi = qr * qr - qi * qi, 2.0 * qr * qi
        d *= 2
    x0 = jnp.where(kidx >= 1, pltpu.roll(x, 1, 0), 0.0)
    y = y + _dot(x0.astype(BF16), wout_ref[...])
    for b in range(nb):
        for c in range(G):
            y_ref[b, 0, pl.ds(c, nk, stride=G), :] = y[b * nk:(b + 1) * nk, c * CHUNK:(c + 1) * CHUNK]


def _s5(u4, ldt, lrr, lir, lrc, lic, bx, by, cx, cy):
    B, NG, R, _ = u4.shape
    nk = R // SSM_GROUP
    gspec = lambda shape: pl.BlockSpec((1,) + shape, lambda g: (g, 0, 0))
    ublk = pl.BlockSpec((B, 1, R, CHUNK), lambda g: (0, g, 0, 0))
    W = SSM_GROUP * CHUNK
    return pl.pallas_call(
        functools.partial(_s5_kernel, nb=B, nk=nk), grid=(NG,),
        in_specs=[ublk, gspec((1, 1)), gspec((1, 2 * SSM_STATE)), gspec((1, 2 * SSM_STATE)),
                  gspec((2 * SSM_STATE, 1)), gspec((2 * SSM_STATE, 1)),
                  gspec((SSM_GROUP, 2 * SSM_STATE)), gspec((SSM_GROUP, 2 * SSM_STATE)),
                  gspec((2 * SSM_STATE, SSM_GROUP)), gspec((2 * SSM_STATE, SSM_GROUP))],
        out_specs=ublk,
        out_shape=jax.ShapeDtypeStruct(u4.shape, F32),
        scratch_shapes=[
            pltpu.VMEM((W, W), BF16),
            pltpu.VMEM((W, 2 * SSM_STATE), BF16),
            pltpu.VMEM((2 * SSM_STATE, W), BF16),
            pltpu.VMEM((2 * SSM_STATE, W), F32),
            pltpu.VMEM((SSM_GROUP, W), F32),
            pltpu.VMEM((B * nk, W), BF16),
        ],
        compiler_params=_params("parallel"), name="s5",
    )(u4, ldt, lrr, lir, lrc, lic, bx, by, cx, cy)


def _memkv_kernel(mem_ref, g_ref, w_ref, k_ref, v_ref):
    mn = _rms_rows(mem_ref[0], g_ref[...]).astype(BF16)
    kv = _dot(mn, w_ref[...])
    d = k_ref.shape[-1]
    k_ref[0] = kv[:, :d].astype(BF16)
    v_ref[0] = kv[:, d:].astype(BF16)


def _memkv(mem, g, w):
    B, M, D = mem.shape
    blk = pl.BlockSpec((1, M, D), lambda b: (b, 0, 0))
    return pl.pallas_call(
        _memkv_kernel, grid=(B,),
        in_specs=[blk, _const_spec(g.shape), _const_spec(w.shape)],
        out_specs=[blk, blk],
        out_shape=[jax.ShapeDtypeStruct((B, M, D), BF16)] * 2,
        compiler_params=_params("parallel"), name="memkv",
    )(mem, g, w)


def _mix_kernel(h_ref, a_ref, yt_ref, ut_ref, d_ref, wglut_ref, bglu_ref, ag_ref, sg_ref, woa_ref, wos_ref,
                xg_ref, wxq_ref, kx_ref, vx_ref, wxo_ref, o_ref):
    def channel_major(ref):
        n_ch = ref.shape[1] * SSM_GROUP
        return jnp.concatenate([ref[0, :, dk * SSM_GROUP:(dk + 1) * SSM_GROUP, :].reshape(n_ch, CHUNK)
                                for dk in range(ref.shape[2] // SSM_GROUP)], axis=1)

    y = channel_major(yt_ref) + d_ref[...] * channel_major(ut_ref)
    g = jax.nn.gelu(y)
    z = _dot(wglut_ref[...], g.astype(BF16)) + bglu_ref[...]
    s = y * jax.nn.sigmoid(z)
    s_n = s * lax.rsqrt(jnp.mean(s * s, axis=0, keepdims=True) + EPS) * sg_ref[...]
    a_n = _rms_rows(a_ref[0].astype(F32), ag_ref[...])
    h1 = h_ref[0] + _dot(a_n.astype(BF16), woa_ref[...]) + _dot(s_n.T.astype(BF16), wos_ref[...])
    qx = _dot(_rms_rows(h1, xg_ref[...]).astype(BF16), wxq_ref[...])
    dh = qx.shape[-1] // X_HEADS
    outs = []
    for hd in range(X_HEADS):
        sl = slice(hd * dh, (hd + 1) * dh)
        sc = _dot_nt(qx[:, sl].astype(BF16), kx_ref[0, :, sl]) * (dh ** -0.5)
        p = jnp.exp(sc - jnp.max(sc, axis=-1, keepdims=True))
        p = p / jnp.sum(p, axis=-1, keepdims=True)
        outs.append(_dot(p.astype(BF16), vx_ref[0, :, sl]))
    o = jnp.concatenate(outs, axis=-1)
    o_ref[0] = h1 + _dot(o.astype(BF16), wxo_ref[...])


def _mix(h, a_out, yt, ut, d, wglut, bglu, ag, sg, woa, wos, xg, wxq, kx, vx, wxo, *, ts):
    B, S, D = h.shape
    M = kx.shape[1]
    row = lambda b, i: (b, i, 0)
    colmaj = pl.BlockSpec((1, yt.shape[1], ts // CHUNK * SSM_GROUP, CHUNK), lambda b, i: (b, 0, i, 0))
    mem = pl.BlockSpec((1, M, D), lambda b, i: (b, 0, 0))
    consts = [d, wglut, bglu, ag, sg, woa, wos, xg, wxq]
    return pl.pallas_call(
        _mix_kernel, grid=(B, S // ts),
        in_specs=[pl.BlockSpec((1, ts, D), row), pl.BlockSpec((1, ts, a_out.shape[-1]), row), colmaj, colmaj]
                 + [_const_spec(c.shape) for c in consts] + [mem, mem, _const_spec(wxo.shape)],
        out_specs=pl.BlockSpec((1, ts, D), row),
        out_shape=jax.ShapeDtypeStruct((B, S, D), F32),
        compiler_params=_params("parallel", "parallel"), name="mix",
    )(h, a_out, yt, ut, *consts, kx, vx, wxo)


def _swiglu_kernel(h_ref, g_ref, wg_ref, wu_ref, wd_ref, fg_ref, o_ref, *, final_norm, chunks):
    h = h_ref[...]
    hn = _rms_rows(h, g_ref[...]).astype(BF16)
    out = h
    lo = 0
    for width in chunks:
        gate = _dot(hn, wg_ref[:, lo:lo + width])
        up = _dot(hn, wu_ref[:, lo:lo + width])
        out = out + _dot((gate * jax.nn.sigmoid(gate) * up).astype(BF16), wd_ref[lo:lo + width, :])
        lo += width
    if final_norm:
        out = _rms_rows(out, fg_ref[...])
    o_ref[...] = out


def _resident_spec(shape):
    nd = len(shape)
    return pl.BlockSpec(shape, lambda *_: (0,) * nd, pipeline_mode=pl.Buffered(1))


def _swiglu(h2d, g, wg, wu, wd, fg, *, tm, chunks, final_norm):
    T, D = h2d.shape
    return pl.pallas_call(
        functools.partial(_swiglu_kernel, final_norm=final_norm, chunks=chunks), grid=(T // tm,),
        in_specs=[pl.BlockSpec((tm, D), lambda i: (i, 0)), _const_spec(g.shape),
                  _resident_spec(wg.shape), _resident_spec(wu.shape), _resident_spec(wd.shape),
                  _const_spec(fg.shape)],
        out_specs=pl.BlockSpec((tm, D), lambda i: (i, 0)),
        out_shape=jax.ShapeDtypeStruct((T, D), F32),
        compiler_params=_params("parallel"), name="swiglu",
    )(h2d, g, wg, wu, wd, fg)


def _rope_swap(w):
    half = w.shape[-1] // 2
    return jnp.concatenate([-w[..., half:], w[..., :half]], axis=-1)


def _tile_of(d, prefs):
    for t in prefs:
        if d % t == 0:
            return t
    return d


def _split(total, target, align):
    n = -(-total // target)
    units = -(-total // align)
    widths = [(units // n + (1 if i < units % n else 0)) * align for i in range(n)]
    widths[-1] -= sum(widths) - total
    return tuple(w for w in widths if w > 0)


def kernel(x, mem, positions, norm_mix_g, w_in, q_norm_g, w_uq, kv_norm_g, w_ukv, ssm_lambda_re, ssm_lambda_im,
           ssm_log_dt, ssm_b_re, ssm_b_im, ssm_c_re, ssm_c_im, ssm_d, ssm_w_glu, ssm_b_glu, attn_out_g, ssm_out_g,
           w_out, norm_x_g, mem_norm_g, w_xq, w_xkv, w_xo, norm_ffn_g, w_gate, w_up, w_down, final_norm_g):
    B, S, D = x.shape
    depth = w_in.shape[0]
    n_ssm = ssm_d.shape[-1]
    n_groups = n_ssm // SSM_GROUP
    mla_w = HEADS * VDIM
    assert S % CHUNK == 0
    nk = S // CHUNK
    tm = _tile_of(S, (512, 256, 128))
    ff_chunks = _split(w_gate.shape[-1], 4 * MXU_WIDTH, MXU_WIDTH)

    pos3 = positions.reshape(B, S, 1)
    freqs = ROPE_THETA ** (-jnp.arange(0, ROPE, 2, dtype=F32) / ROPE)
    freq128 = jnp.zeros((1, LANES), F32).at[0, NOPE:NOPE + ROPE].set(jnp.concatenate([freqs, freqs]))
    zeros = lambda *s: jnp.zeros(s, F32)

    h = x
    for l in range(depth):
        wi = w_in[l]
        w_kr = wi[:, Q_LORA + KV_LORA:Q_LORA + KV_LORA + ROPE]
        pad_rope = lambda w: jnp.concatenate([zeros(D, NOPE), w, zeros(D, HEAD_PAD - NOPE - ROPE)], axis=1)
        wcat = jnp.concatenate([wi[:, :Q_LORA + KV_LORA], pad_rope(w_kr), pad_rope(_rope_swap(w_kr))], axis=1).astype(BF16)
        wut = wi[:, Q_LORA + KV_LORA + ROPE:].T.astype(BF16)
        wq = w_uq[l].reshape(Q_LORA, HEADS, NOPE + ROPE)
        zq = zeros(Q_LORA, HEADS, HEAD_PAD - NOPE - ROPE)
        wqm = jnp.concatenate([wq, zq], axis=-1).reshape(Q_LORA, HEADS * HEAD_PAD).astype(BF16)
        wqs = jnp.concatenate([zeros(Q_LORA, HEADS, NOPE), _rope_swap(wq[..., NOPE:]), zq], axis=-1)
        wqs = wqs.reshape(Q_LORA, HEADS * HEAD_PAD).astype(BF16)
        wkv = w_ukv[l].reshape(KV_LORA, HEADS, NOPE + VDIM)
        wk = jnp.concatenate([wkv[..., :NOPE], zeros(KV_LORA, HEADS, HEAD_PAD - NOPE)], axis=-1)
        wk = wk.reshape(KV_LORA, HEADS * HEAD_PAD).astype(BF16)
        wv2 = wkv[..., NOPE:].reshape(KV_LORA, HEADS // 2, 2, VDIM)
        zv = zeros(KV_LORA, HEADS // 2, HEAD_PAD - VDIM)
        wv = jnp.concatenate([wv2[:, :, 0], zv, zv, wv2[:, :, 1]], axis=-1)
        wv = wv.reshape(KV_LORA, HEADS * HEAD_PAD).astype(BF16)

        q, k, v, ut = _front(h, pos3, freq128, norm_mix_g[l][None], wcat, wut, q_norm_g[l][None], wqm, wqs,
                             kv_norm_g[l][None], wk, wv, tm=tm)
        a_out = _mla(q, k, v, tq=tm, tk=min(MLA_TK, tm), unroll=MLA_UNROLL)

        lr = ssm_lambda_re[l]
        li = ssm_lambda_im[l]
        dup = lambda a: jnp.concatenate([a, a], axis=-1)
        brt = jnp.swapaxes(ssm_b_re[l], 1, 2)
        bit = jnp.swapaxes(ssm_b_im[l], 1, 2)
        crt = jnp.swapaxes(ssm_c_re[l], 1, 2)
        cit = jnp.swapaxes(ssm_c_im[l], 1, 2)
        yt = _s5(ut, ssm_log_dt[l].reshape(n_groups, 1, 1),
                 dup(lr)[:, None, :], dup(li)[:, None, :], dup(lr)[:, :, None], dup(li)[:, :, None],
                 jnp.concatenate([brt, bit], axis=-1), jnp.concatenate([-bit, brt], axis=-1),
                 jnp.concatenate([crt, cit], axis=1), jnp.concatenate([-cit, crt], axis=1))

        kx, vx = _memkv(mem, mem_norm_g[l][None], w_xkv[l].astype(BF16))
        h = _mix(h, a_out, yt, ut, ssm_d[l][:, None], ssm_w_glu[l].T.astype(BF16), ssm_b_glu[l][:, None],
                 attn_out_g[l][None], ssm_out_g[l][:, None], w_out[l][:mla_w].astype(BF16),
                 w_out[l][mla_w:].astype(BF16), norm_x_g[l][None], w_xq[l].astype(BF16), kx, vx,
                 w_xo[l].astype(BF16), ts=tm)
        h = _swiglu(h.reshape(B * S, D), norm_ffn_g[l][None], w_gate[l].astype(BF16), w_up[l].astype(BF16),
                    w_down[l].astype(BF16), final_norm_g[None], tm=tm, chunks=ff_chunks,
                    final_norm=(l == depth - 1)).reshape(B, S, D)
    return h
```

```python
import functools
import math

import jax
import jax.numpy as jnp
from jax import lax
from jax.experimental import pallas as pl
from jax.experimental.pallas import tpu as pltpu

LANES = 128
MXU_WIDTH = 256
VMEM_LIMIT_BYTES = 56 * 1024 * 1024

EPS = 1e-6
HEADS = 8
NOPE = 64
ROPE = 32
VDIM = 64
HEAD_PAD = LANES
V_SUM_LANE = (VDIM, 0)
Q_LORA = 256
KV_LORA = 128
ROPE_THETA = 10000.0
SSM_GROUP = 16
SSM_STATE = 64
CHUNK = LANES
X_HEADS = 4
MLA_TK = 512
MLA_UNROLL = 7
MLA_HEADS_PER_STEP = 2

F32 = jnp.float32
BF16 = jnp.bfloat16
NEG = -1e30


def _dot(a, b):
    return jnp.dot(a, b, preferred_element_type=F32)


def _dot_nt(a, b):
    return lax.dot_general(a, b, (((1,), (1,)), ((), ())), preferred_element_type=F32)


def _rms_rows(x, g):
    return x * lax.rsqrt(jnp.mean(x * x, axis=-1, keepdims=True) + EPS) * g


def _params(*sem):
    return pltpu.CompilerParams(dimension_semantics=sem, vmem_limit_bytes=VMEM_LIMIT_BYTES)


def _const_spec(shape):
    nd = len(shape)
    return pl.BlockSpec(shape, lambda *_: (0,) * nd)


def _front_kernel(h_ref, pos_ref, freq_ref, gmix_ref, wcat_ref, wut_ref, qg_ref, wqm_ref, wqs_ref,
                  kvg_ref, wk_ref, wv_ref, q_ref, k_ref, v_ref, ut_ref):
    h = h_ref[0]
    xn = _rms_rows(h, gmix_ref[...]).astype(BF16)
    proj = _dot(xn, wcat_ref[...])
    ut = _dot_nt(wut_ref[...], xn)
    n_groups = ut_ref.shape[1]
    for dk in range(ut.shape[1] // CHUNK):
        ut_ref[0, :, dk * SSM_GROUP:(dk + 1) * SSM_GROUP, :] = (
            ut[:, dk * CHUNK:(dk + 1) * CHUNK].reshape(n_groups, SSM_GROUP, CHUNK))
    cq = proj[:, :Q_LORA]
    ckv = proj[:, Q_LORA:Q_LORA + KV_LORA]
    kr_main = proj[:, Q_LORA + KV_LORA:Q_LORA + KV_LORA + LANES]
    kr_swap = proj[:, Q_LORA + KV_LORA + LANES:]
    ang = pos_ref[0].astype(F32) * freq_ref[...]
    cos = jnp.cos(ang)
    sin = jnp.sin(ang)
    scale = (NOPE + ROPE) ** -0.5 * math.log2(math.e)
    cqn = _rms_rows(cq, qg_ref[...]).astype(BF16)
    ckvn = _rms_rows(ckv, kvg_ref[...]).astype(BF16)
    qm = _dot(cqn, wqm_ref[...])
    qs = _dot(cqn, wqs_ref[...])
    cos_q = jnp.tile(cos * scale, (1, HEADS))
    sin_q = jnp.tile(sin * scale, (1, HEADS))
    q_ref[0] = (qm * cos_q + qs * sin_q).astype(BF16)
    k_rot = kr_main * cos + kr_swap * sin
    k_ref[0] = (_dot(ckvn, wk_ref[...]) + jnp.tile(k_rot, (1, HEADS))).astype(BF16)
    vlane = lax.broadcasted_iota(jnp.int32, (1, HEADS * HEAD_PAD), 1) % (2 * HEAD_PAD)
    ones = jnp.where((vlane == V_SUM_LANE[0]) | (vlane == HEAD_PAD + V_SUM_LANE[1]), 1.0, 0.0).astype(F32)
    v_ref[0] = (_dot(ckvn, wv_ref[...]) + ones).astype(BF16)


def _front(h, pos3, freq, gmix, wcat, wut, qg, wqm, wqs, kvg, wk, wv, *, tm):
    B, S, D = h.shape
    n_ssm = wut.shape[0]
    grid = (B, S // tm)
    row = lambda b, i: (b, i, 0)
    in_specs = [
        pl.BlockSpec((1, tm, D), row),
        pl.BlockSpec((1, tm, 1), row),
        _const_spec(freq.shape), _const_spec(gmix.shape), _const_spec(wcat.shape), _const_spec(wut.shape),
        _const_spec(qg.shape), _const_spec(wqm.shape), _const_spec(wqs.shape),
        _const_spec(kvg.shape), _const_spec(wk.shape), _const_spec(wv.shape),
    ]
    out_specs = [
        pl.BlockSpec((1, tm, HEADS * HEAD_PAD), row),
        pl.BlockSpec((1, tm, HEADS * HEAD_PAD), row),
        pl.BlockSpec((1, tm, HEADS * HEAD_PAD), row),
        pl.BlockSpec((1, n_ssm // SSM_GROUP, tm // CHUNK * SSM_GROUP, CHUNK), lambda b, i: (b, 0, i, 0)),
    ]
    out_shape = [
        jax.ShapeDtypeStruct((B, S, HEADS * HEAD_PAD), BF16),
        jax.ShapeDtypeStruct((B, S, HEADS * HEAD_PAD), BF16),
        jax.ShapeDtypeStruct((B, S, HEADS * HEAD_PAD), BF16),
        jax.ShapeDtypeStruct((B, n_ssm // SSM_GROUP, S // CHUNK * SSM_GROUP, CHUNK), F32),
    ]
    return pl.pallas_call(
        _front_kernel, grid=grid, in_specs=in_specs, out_specs=out_specs, out_shape=out_shape,
        compiler_params=_params("parallel", "parallel"), name="front",
    )(h, pos3, freq, gmix, wcat, wut, qg, wqm, wqs, kvg, wk, wv)


def _mla_kernel(qi_tab, kj_tab, bias_tab, q_ref, k_ref, v_ref, o_ref, bias_ref, *, tq, tk, n_steps, unroll):
    n = n_steps
    NH = q_ref.shape[2] // HEAD_PAD
    row = lax.broadcasted_iota(jnp.int32, (tq, tk), 0)
    col = lax.broadcasted_iota(jnp.int32, (tq, tk), 1)
    bias_ref[0] = jnp.zeros((tq, tk), F32)
    for d in range(tq // tk):
        bias_ref[1 + d] = jnp.where(col + d * tk <= row, 0.0, NEG).astype(F32)
    lane = lax.broadcasted_iota(jnp.int32, (tq, HEAD_PAD), 1)

    def rows(ref, r, size, hh):
        return ref[0, pl.ds(pl.multiple_of(r * size, size), size), hh * HEAD_PAD:(hh + 1) * HEAD_PAD]

    def scores(t):
        bias = bias_ref[bias_tab[t]]
        return [_dot_nt(rows(q_ref, qi_tab[t], tq, hh), rows(k_ref, kj_tab[t], tk, hh)) + bias for hh in range(NH)]

    def finish(t, accs, ps):
        accs = [accs[hh] + _dot(ps[hh], rows(v_ref, kj_tab[t], tk, hh)) for hh in range(NH)]
        o = [accs[hh] / accs[hh][:, V_SUM_LANE[hh % 2]:V_SUM_LANE[hh % 2] + 1] for hh in range(NH)]
        start = pl.multiple_of(qi_tab[t] * tq, tq)
        for pr in range(NH // 2):
            o_ref[0, pl.ds(start, tq), pr * HEAD_PAD:(pr + 1) * HEAD_PAD] = (
                jnp.where(lane < VDIM, o[2 * pr], o[2 * pr + 1]).astype(o_ref.dtype))
        return accs

    def softmax(t, ss, ms, accs):
        fresh = kj_tab[t] == 0
        new_ms, new_accs, new_ps = [], [], []
        for hh in range(NH):
            m_in = jnp.where(fresh, NEG, ms[hh])
            m_new = jnp.maximum(m_in, jnp.max(ss[hh], axis=-1, keepdims=True))
            new_ps.append(jnp.exp2(ss[hh] - m_new).astype(BF16))
            new_ms.append(m_new)
            new_accs.append(jnp.exp2(m_in - m_new) * accs[hh])
        return new_ms, new_accs, new_ps

    def trip(t, state):
        ms, accs, ps = state
        ss = scores(t)
        accs = finish(t - 1, accs, ps)
        return softmax(t, ss, ms, accs)

    def trips(i, state):
        for u in range(unroll):
            state = trip(1 + i * unroll + u, state)
        return state

    zero = jnp.zeros((tq, HEAD_PAD), F32)
    neg = jnp.full((tq, 1), NEG, F32)
    state = softmax(0, scores(0), [neg] * NH, [zero] * NH)
    looped = (n - 1) // unroll
    state = lax.fori_loop(0, looped, trips, state)
    for t in range(1 + looped * unroll, n):
        state = trip(t, state)
    finish(n - 1, state[1], state[2])


def _mla(q, k, v, *, tq, tk, unroll):
    B, S, _ = q.shape
    per = tq // tk
    steps = [(i, j) for i in range(S // tq) for j in range((i + 1) * per)]
    tab = lambda f: jnp.asarray([f(i, j) for i, j in steps], jnp.int32)
    qi_tab, kj_tab = tab(lambda i, j: i), tab(lambda i, j: j)
    bias_tab = tab(lambda i, j: max(0, j - i * per + 1))
    pair = lambda width: pl.BlockSpec((1, S, width), lambda b, p, *_: (b, 0, p))
    grid_spec = pltpu.PrefetchScalarGridSpec(
        num_scalar_prefetch=3, grid=(B, HEADS // MLA_HEADS_PER_STEP),
        in_specs=[pair(MLA_HEADS_PER_STEP * HEAD_PAD)] * 3,
        out_specs=pair(MLA_HEADS_PER_STEP * VDIM),
        scratch_shapes=[pltpu.VMEM((1 + per, tq, tk), F32)])
    return pl.pallas_call(
        functools.partial(_mla_kernel, tq=tq, tk=tk, n_steps=len(steps), unroll=unroll), grid_spec=grid_spec,
        out_shape=jax.ShapeDtypeStruct((B, S, HEADS * VDIM), BF16),
        compiler_params=_params("parallel", "parallel"), name="mla",
    )(qi_tab, kj_tab, bias_tab, q, k, v)


def _cpow(zr, zi, e):
    mag = jnp.exp(e * zr)
    return mag * jnp.cos(e * zi), mag * jnp.sin(e * zi)


def _s5_kernel(u_ref, ldt_ref, lrr_ref, lir_ref, lrc_ref, lic_ref, bx_ref, by_ref, cx_ref, cy_ref,
               y_ref, tt_ref, wst_ref, wout_ref, ep_ref, km_ref, lhs_ref, *, nb, nk):
    P2 = 2 * SSM_STATE
    G = SSM_GROUP
    dt = jnp.exp(ldt_ref[0])
    lr = lrr_ref[0]
    li = lir_ref[0]
    zr = lr * dt
    zi = li * dt
    ar, ai = _cpow(zr, zi, 1.0)
    den = lr * lr + li * li
    cr = ((ar - 1.0) * lr + ai * li) / den
    ci = (ai * lr - (ar - 1.0) * li) / den
    bx = bx_ref[0]
    by = by_ref[0]
    bbx = cr * bx + ci * by
    bby = cr * by - ci * bx
    lane = lax.broadcasted_iota(jnp.int32, (1, P2), 1)
    sign_lane = jnp.where(lane < SSM_STATE, 1.0, -1.0).astype(F32)
    e_rows = (CHUNK - 1 - lax.broadcasted_iota(jnp.int32, (CHUNK, 1), 0)).astype(F32)
    pr, pi = _cpow(zr, zi, e_rows)
    for c in range(G):
        wst_ref[c * CHUNK:(c + 1) * CHUNK, :] = (pr * bbx[c:c + 1, :] + pi * bby[c:c + 1, :]).astype(BF16)
    zrc = lrc_ref[0] * dt
    zic = lic_ref[0] * dt
    tau = lax.broadcasted_iota(jnp.int32, (1, CHUNK), 1).astype(F32)
    a0r, a0i = _cpow(zrc, zic, tau)
    a1r, a1i = _cpow(zrc, zic, tau + 1.0)
    cx = cx_ref[0]
    cy = cy_ref[0]
    sub = lax.broadcasted_iota(jnp.int32, (P2, 1), 0)
    sign_sub = jnp.where(sub < SSM_STATE, 1.0, -1.0).astype(F32)
    for c in range(G):
        cxc = cx[:, c:c + 1]
        cyc = cy[:, c:c + 1]
        ep_ref[:, c * CHUNK:(c + 1) * CHUNK] = a0r * cxc + a0i * cyc
        wout_ref[:, c * CHUNK:(c + 1) * CHUNK] = ((a1r * cxc + a1i * cyc) * sign_sub).astype(BF16)
    km_ref[...] = jnp.dot(bbx * sign_lane, ep_ref[...], preferred_element_type=F32,
                          precision=lax.Precision.HIGHEST)
    jj = lax.broadcasted_iota(jnp.int32, (CHUNK, CHUNK), 0)
    tcol = lax.broadcasted_iota(jnp.int32, (CHUNK, CHUNK), 1)
    lower = tcol >= jj

    def toeplitz_rows(cp, _):
        krow = km_ref[pl.ds(cp, 1), :]
        r0 = pl.multiple_of(cp * CHUNK, CHUNK)
        for c in range(G):
            blk = jnp.broadcast_to(krow[:, c * CHUNK:(c + 1) * CHUNK], (CHUNK, CHUNK))
            blk = pltpu.roll(blk, 0, 1, stride=1, stride_axis=0)
            tt_ref[pl.ds(r0, CHUNK), c * CHUNK:(c + 1) * CHUNK] = jnp.where(lower, blk, 0.0).astype(BF16)
        return 0

    lax.fori_loop(0, G, toeplitz_rows, 0)
    for b in range(nb):
        for c in range(G):
            lhs_ref[b * nk:(b + 1) * nk, c * CHUNK:(c + 1) * CHUNK] = (
                u_ref[b, 0, pl.ds(c, nk, stride=G), :].astype(BF16))
    lhs = lhs_ref[...]
    y = _dot(lhs, tt_ref[...])
    x = _dot(lhs, wst_ref[...])
    kidx = lax.broadcasted_iota(jnp.int32, (nb * nk, 1), 0) % nk
    qr, qi = ar, ai
    for _ in range(int(math.log2(CHUNK))):
        qr, qi = qr * qr - qi * qi, 2.0 * qr * qi
    d = 1
    while d < nk:
        sx = jnp.where(kidx >= d, pltpu.roll(x, d, 0), 0.0)
        x = x + qr * sx + (qi * -sign_lane) * pltpu.roll(sx, SSM_STATE, 1)
        qr, qi = qr * qr - qi * qi, 2.0 * qr * qi
        d *= 2
    x0 = jnp.where(kidx >= 1, pltpu.roll(x, 1, 0), 0.0)
    y = y + _dot(x0.astype(BF16), wout_ref[...])
    for b in range(nb):
        for c in range(G):
            y_ref[b, 0, pl.ds(c, nk, stride=G), :] = y[b * nk:(b + 1) * nk, c * CHUNK:(c + 1) * CHUNK]


def _s5(u4, ldt, lrr, lir, lrc, lic, bx, by, cx, cy):
    B, NG, R, _ = u4.shape
    nk = R // SSM_GROUP
    gspec = lambda shape: pl.BlockSpec((1,) + shape, lambda g: (g, 0, 0))
    ublk = pl.BlockSpec((B, 1, R, CHUNK), lambda g: (0, g, 0, 0))
    W = SSM_GROUP * CHUNK
    return pl.pallas_call(
        functools.partial(_s5_kernel, nb=B, nk=nk), grid=(NG,),
        in_specs=[ublk, gspec((1, 1)), gspec((1, 2 * SSM_STATE)), gspec((1, 2 * SSM_STATE)),
                  gspec((2 * SSM_STATE, 1)), gspec((2 * SSM_STATE, 1)),
                  gspec((SSM_GROUP, 2 * SSM_STATE)), gspec((SSM_GROUP, 2 * SSM_STATE)),
                  gspec((2 * SSM_STATE, SSM_GROUP)), gspec((2 * SSM_STATE, SSM_GROUP))],
        out_specs=ublk,
        out_shape=jax.ShapeDtypeStruct(u4.shape, F32),
        scratch_shapes=[
            pltpu.VMEM((W, W), BF16),
            pltpu.VMEM((W, 2 * SSM_STATE), BF16),
            pltpu.VMEM((2 * SSM_STATE, W), BF16),
            pltpu.VMEM((2 * SSM_STATE, W), F32),
            pltpu.VMEM((SSM_GROUP, W), F32),
            pltpu.VMEM((B * nk, W), BF16),
        ],
        compiler_params=_params("parallel"), name="s5",
    )(u4, ldt, lrr, lir, lrc, lic, bx, by, cx, cy)


def _memkv_kernel(mem_ref, g_ref, w_ref, k_ref, v_ref):
    mn = _rms_rows(mem_ref[0], g_ref[...]).astype(BF16)
    kv = _dot(mn, w_ref[...])
    d = k_ref.shape[-1]
    k_ref[0] = kv[:, :d].astype(BF16)
    v_ref[0] = kv[:, d:].astype(BF16)


def _memkv(mem, g, w):
    B, M, D = mem.shape
    blk = pl.BlockSpec((1, M, D), lambda b: (b, 0, 0))
    return pl.pallas_call(
        _memkv_kernel, grid=(B,),
        in_specs=[blk, _const_spec(g.shape), _const_spec(w.shape)],
        out_specs=[blk, blk],
        out_shape=[jax.ShapeDtypeStruct((B, M, D), BF16)] * 2,
        compiler_params=_params("parallel"), name="memkv",
    )(mem, g, w)


def _mix_kernel(h_ref, a_ref, yt_ref, ut_ref, d_ref, wglut_ref, bglu_ref, ag_ref, sg_ref, woa_ref, wos_ref,
                xg_ref, wxq_ref, kx_ref, vx_ref, wxo_ref, o_ref):
    def channel_major(ref):
        n_ch = ref.shape[1] * SSM_GROUP
        return jnp.concatenate([ref[0, :, dk * SSM_GROUP:(dk + 1) * SSM_GROUP, :].reshape(n_ch, CHUNK)
                                for dk in range(ref.shape[2] // SSM_GROUP)], axis=1)

    a_n = _rms_rows(a_ref[0].astype(F32), ag_ref[...])
    h0 = h_ref[0] + _dot(a_n.astype(BF16), woa_ref[...])
    y = channel_major(yt_ref) + d_ref[...] * channel_major(ut_ref)
    z = _dot(wglut_ref[...], jax.nn.gelu(y).astype(BF16)) + bglu_ref[...]
    s = y * jax.nn.sigmoid(z)
    s_n = s * lax.rsqrt(jnp.mean(s * s, axis=0, keepdims=True) + EPS) * sg_ref[...]
    h1 = h0 + _dot(s_n.T.astype(BF16), wos_ref[...])
    qx = _dot(_rms_rows(h1, xg_ref[...]).astype(BF16), wxq_ref[...]).astype(BF16)
    dh = qx.shape[-1] // X_HEADS
    outs = []
    for hd in range(X_HEADS):
        sl = slice(hd * dh, (hd + 1) * dh)
        sc = _dot_nt(qx[:, sl], kx_ref[0, :, sl])
        p = jnp.exp(sc - jnp.max(sc, axis=-1, keepdims=True))
        p = p * (1.0 / jnp.sum(p, axis=-1, keepdims=True))
        outs.append(_dot(p.astype(BF16), vx_ref[0, :, sl]))
    o_ref[0] = h1 + _dot(jnp.concatenate(outs, axis=-1).astype(BF16), wxo_ref[...])


def _mix(h, a_out, yt, ut, d, wglut, bglu, ag, sg, woa, wos, xg, wxq, kx, vx, wxo, *, ts):
    B, S, D = h.shape
    M = kx.shape[1]
    row = lambda b, i: (b, i, 0)
    colmaj = pl.BlockSpec((1, yt.shape[1], ts // CHUNK * SSM_GROUP, CHUNK), lambda b, i: (b, 0, i, 0))
    mem = pl.BlockSpec((1, M, D), lambda b, i: (b, 0, 0))
    consts = [d, wglut, bglu, ag, sg, woa, wos, xg, wxq]
    return pl.pallas_call(
        _mix_kernel, grid=(B, S // ts),
        in_specs=[pl.BlockSpec((1, ts, D), row), pl.BlockSpec((1, ts, a_out.shape[-1]), row), colmaj, colmaj]
                 + [_const_spec(c.shape) for c in consts] + [mem, mem, _const_spec(wxo.shape)],
        out_specs=pl.BlockSpec((1, ts, D), row),
        out_shape=jax.ShapeDtypeStruct((B, S, D), F32),
        compiler_params=_params("parallel", "parallel"), name="mix",
    )(h, a_out, yt, ut, *consts, kx, vx, wxo)


def _swiglu_kernel(h_ref, g_ref, wg_ref, wu_ref, wd_ref, fg_ref, o_ref, *, final_norm, chunks):
    h = h_ref[...]
    hn = _rms_rows(h, g_ref[...]).astype(BF16)
    out = h
    lo = 0
    for width in chunks:
        gate = _dot(hn, wg_ref[:, lo:lo + width])
        up = _dot(hn, wu_ref[:, lo:lo + width])
        out = out + _dot((gate * jax.nn.sigmoid(gate) * up).astype(BF16), wd_ref[lo:lo + width, :])
        lo += width
    if final_norm:
        out = _rms_rows(out, fg_ref[...])
    o_ref[...] = out


def _resident_spec(shape):
    nd = len(shape)
    return pl.BlockSpec(shape, lambda *_: (0,) * nd, pipeline_mode=pl.Buffered(1))


def _swiglu(h2d, g, wg, wu, wd, fg, *, tm, chunks, final_norm):
    T, D = h2d.shape
    return pl.pallas_call(
        functools.partial(_swiglu_kernel, final_norm=final_norm, chunks=chunks), grid=(T // tm,),
        in_specs=[pl.BlockSpec((tm, D), lambda i: (i, 0)), _const_spec(g.shape),
                  _resident_spec(wg.shape), _resident_spec(wu.shape), _resident_spec(wd.shape),
                  _const_spec(fg.shape)],
        out_specs=pl.BlockSpec((tm, D), lambda i: (i, 0)),
        out_shape=jax.ShapeDtypeStruct((T, D), F32),
        compiler_params=_params("parallel"), name="swiglu",
    )(h2d, g, wg, wu, wd, fg)


def _rope_swap(w):
    half = w.shape[-1] // 2
    return jnp.concatenate([-w[..., half:], w[..., :half]], axis=-1)


def _tile_of(d, prefs):
    for t in prefs:
        if d % t == 0:
            return t
    return d


def _split(total, target, align):
    n = -(-total // target)
    units = -(-total // align)
    widths = [(units // n + (1 if i < units % n else 0)) * align for i in range(n)]
    widths[-1] -= sum(widths) - total
    return tuple(w for w in widths if w > 0)


def kernel(x, mem, positions, norm_mix_g, w_in, q_norm_g, w_uq, kv_norm_g, w_ukv, ssm_lambda_re, ssm_lambda_im,
           ssm_log_dt, ssm_b_re, ssm_b_im, ssm_c_re, ssm_c_im, ssm_d, ssm_w_glu, ssm_b_glu, attn_out_g, ssm_out_g,
           w_out, norm_x_g, mem_norm_g, w_xq, w_xkv, w_xo, norm_ffn_g, w_gate, w_up, w_down, final_norm_g):
    B, S, D = x.shape
    depth = w_in.shape[0]
    n_ssm = ssm_d.shape[-1]
    n_groups = n_ssm // SSM_GROUP
    mla_w = HEADS * VDIM
    assert S % CHUNK == 0
    nk = S // CHUNK
    tm = _tile_of(S, (512, 256, 128))
    ff_chunks = _split(w_gate.shape[-1], 4 * MXU_WIDTH, MXU_WIDTH)

    pos3 = positions.reshape(B, S, 1)
    freqs = ROPE_THETA ** (-jnp.arange(0, ROPE, 2, dtype=F32) / ROPE)
    freq128 = jnp.zeros((1, LANES), F32).at[0, NOPE:NOPE + ROPE].set(jnp.concatenate([freqs, freqs]))
    zeros = lambda *s: jnp.zeros(s, F32)

    h = x
    for l in range(depth):
        wi = w_in[l]
        w_kr = wi[:, Q_LORA + KV_LORA:Q_LORA + KV_LORA + ROPE]
        pad_rope = lambda w: jnp.concatenate([zeros(D, NOPE), w, zeros(D, HEAD_PAD - NOPE - ROPE)], axis=1)
        wcat = jnp.concatenate([wi[:, :Q_LORA + KV_LORA], pad_rope(w_kr), pad_rope(_rope_swap(w_kr))], axis=1).astype(BF16)
        wut = wi[:, Q_LORA + KV_LORA + ROPE:].T.astype(BF16)
        wq = w_uq[l].reshape(Q_LORA, HEADS, NOPE + ROPE)
        zq = zeros(Q_LORA, HEADS, HEAD_PAD - NOPE - ROPE)
        wqm = jnp.concatenate([wq, zq], axis=-1).reshape(Q_LORA, HEADS * HEAD_PAD).astype(BF16)
        wqs = jnp.concatenate([zeros(Q_LORA, HEADS, NOPE), _rope_swap(wq[..., NOPE:]), zq], axis=-1)
        wqs = wqs.reshape(Q_LORA, HEADS * HEAD_PAD).astype(BF16)
        wkv = w_ukv[l].reshape(KV_LORA, HEADS, NOPE + VDIM)
        wk = jnp.concatenate([wkv[..., :NOPE], zeros(KV_LORA, HEADS, HEAD_PAD - NOPE)], axis=-1)
        wk = wk.reshape(KV_LORA, HEADS * HEAD_PAD).astype(BF16)
        wv2 = wkv[..., NOPE:].reshape(KV_LORA, HEADS // 2, 2, VDIM)
        zv = zeros(KV_LORA, HEADS // 2, HEAD_PAD - VDIM)
        wv = jnp.concatenate([wv2[:, :, 0], zv, zv, wv2[:, :, 1]], axis=-1)
        wv = wv.reshape(KV_LORA, HEADS * HEAD_PAD).astype(BF16)

        q, k, v, ut = _front(h, pos3, freq128, norm_mix_g[l][None], wcat, wut, q_norm_g[l][None], wqm, wqs,
                             kv_norm_g[l][None], wk, wv, tm=tm)
        a_out = _mla(q, k, v, tq=tm, tk=min(MLA_TK, tm), unroll=MLA_UNROLL)

        lr = ssm_lambda_re[l]
        li = ssm_lambda_im[l]
        dup = lambda a: jnp.concatenate([a, a], axis=-1)
        brt = jnp.swapaxes(ssm_b_re[l], 1, 2)
        bit = jnp.swapaxes(ssm_b_im[l], 1, 2)
        crt = jnp.swapaxes(ssm_c_re[l], 1, 2)
        cit = jnp.swapaxes(ssm_c_im[l], 1, 2)
        yt = _s5(ut, ssm_log_dt[l].reshape(n_groups, 1, 1),
                 dup(lr)[:, None, :], dup(li)[:, None, :], dup(lr)[:, :, None], dup(li)[:, :, None],
                 jnp.concatenate([brt, bit], axis=-1), jnp.concatenate([-bit, brt], axis=-1),
                 jnp.concatenate([crt, cit], axis=1), jnp.concatenate([-cit, crt], axis=1))

        kx, vx = _memkv(mem, mem_norm_g[l][None], w_xkv[l].astype(BF16))
        h = _mix(h, a_out, yt, ut, ssm_d[l][:, None], ssm_w_glu[l].T.astype(BF16), ssm_b_glu[l][:, None],
                 attn_out_g[l][None], ssm_out_g[l][:, None], w_out[l][:mla_w].astype(BF16),
                 w_out[l][mla_w:].astype(BF16), norm_x_g[l][None],
                 (w_xq[l] * (D // X_HEADS) ** -0.5).astype(BF16), kx, vx, w_xo[l].astype(BF16), ts=tm)
        h = _swiglu(h.reshape(B * S, D), norm_ffn_g[l][None], w_gate[l].astype(BF16), w_up[l].astype(BF16),
                    w_down[l].astype(BF16), final_norm_g[None], tm=_tile_of(B * S, (1024, 512, 256, 128)), chunks=ff_chunks,
                    final_norm=(l == depth - 1)).reshape(B, S, D)
    return h
```

```python
import functools
import math

import jax
import jax.numpy as jnp
from jax import lax
from jax.experimental import pallas as pl
from jax.experimental.pallas import tpu as pltpu

LANES = 128
MXU_WIDTH = 256
VMEM_LIMIT_BYTES = 56 * 1024 * 1024

EPS = 1e-6
HEADS = 8
NOPE = 64
ROPE = 32
VDIM = 64
HEAD_PAD = LANES
V_SUM_LANE = (VDIM, 0)
Q_LORA = 256
KV_LORA = 128
ROPE_THETA = 10000.0
SSM_GROUP = 16
SSM_STATE = 64
CHUNK = LANES
X_HEADS = 4
MLA_TK = 512
MLA_UNROLL = 7
MLA_HEADS_PER_STEP = 2

F32 = jnp.float32
BF16 = jnp.bfloat16
NEG = -1e30


def _dot(a, b):
    return jnp.dot(a, b, preferred_element_type=F32)


def _dot_nt(a, b):
    return lax.dot_general(a, b, (((1,), (1,)), ((), ())), preferred_element_type=F32)


def _rms_rows(x, g):
    return x * lax.rsqrt(jnp.mean(x * x, axis=-1, keepdims=True) + EPS) * g


def _params(*sem):
    return pltpu.CompilerParams(dimension_semantics=sem, vmem_limit_bytes=VMEM_LIMIT_BYTES)


def _layer_spec(arr, l, resident=False):
    nd = arr.ndim
    return pl.BlockSpec((None,) + arr.shape[1:], lambda *_: (l,) + (0,) * (nd - 1),
                        pipeline_mode=pl.Buffered(1) if resident else None)


def _const_spec(shape):
    nd = len(shape)
    return pl.BlockSpec(shape, lambda *_: (0,) * nd)


def _front_kernel(h_ref, pos_ref, freq_ref, gmix_ref, wcat_ref, wut_ref, qg_ref, wqm_ref, wqs_ref,
                  kvg_ref, wk_ref, wv_ref, q_ref, k_ref, v_ref, ut_ref):
    h = h_ref[0]
    xn = _rms_rows(h, gmix_ref[...]).astype(BF16)
    proj = _dot(xn, wcat_ref[...])
    ut = _dot_nt(wut_ref[...], xn)
    n_groups = ut_ref.shape[1]
    for dk in range(ut.shape[1] // CHUNK):
        ut_ref[0, :, dk * SSM_GROUP:(dk + 1) * SSM_GROUP, :] = (
            ut[:, dk * CHUNK:(dk + 1) * CHUNK].reshape(n_groups, SSM_GROUP, CHUNK))
    cq = proj[:, :Q_LORA]
    ckv = proj[:, Q_LORA:Q_LORA + KV_LORA]
    kr_main = proj[:, Q_LORA + KV_LORA:Q_LORA + KV_LORA + LANES]
    kr_swap = proj[:, Q_LORA + KV_LORA + LANES:]
    ang = pos_ref[0].astype(F32) * freq_ref[...]
    cos = jnp.cos(ang)
    sin = jnp.sin(ang)
    scale = (NOPE + ROPE) ** -0.5 * math.log2(math.e)
    cqn = _rms_rows(cq, qg_ref[...]).astype(BF16)
    ckvn = _rms_rows(ckv, kvg_ref[...]).astype(BF16)
    qm = _dot(cqn, wqm_ref[...])
    qs = _dot(cqn, wqs_ref[...])
    cos_q = jnp.tile(cos * scale, (1, HEADS))
    sin_q = jnp.tile(sin * scale, (1, HEADS))
    q_ref[0] = (qm * cos_q + qs * sin_q).astype(BF16)
    k_rot = kr_main * cos + kr_swap * sin
    k_ref[0] = (_dot(ckvn, wk_ref[...]) + jnp.tile(k_rot, (1, HEADS))).astype(BF16)
    vlane = lax.broadcasted_iota(jnp.int32, (1, HEADS * HEAD_PAD), 1) % (2 * HEAD_PAD)
    ones = jnp.where((vlane == V_SUM_LANE[0]) | (vlane == HEAD_PAD + V_SUM_LANE[1]), 1.0, 0.0).astype(F32)
    v_ref[0] = (_dot(ckvn, wv_ref[...]) + ones).astype(BF16)


def _front(h, pos3, freq, gmix, wcat, wut, qg, wqm, wqs, kvg, wk, wv, *, tm, l):
    B, S, D = h.shape
    n_ssm = wut.shape[1]
    grid = (B, S // tm)
    row = lambda b, i: (b, i, 0)
    in_specs = [
        pl.BlockSpec((1, tm, D), row),
        pl.BlockSpec((1, tm, 1), row),
        _const_spec(freq.shape),
    ] + [_layer_spec(w, l) for w in (gmix, wcat, wut, qg, wqm, wqs, kvg, wk, wv)]
    out_specs = [
        pl.BlockSpec((1, tm, HEADS * HEAD_PAD), row),
        pl.BlockSpec((1, tm, HEADS * HEAD_PAD), row),
        pl.BlockSpec((1, tm, HEADS * HEAD_PAD), row),
        pl.BlockSpec((1, n_ssm // SSM_GROUP, tm // CHUNK * SSM_GROUP, CHUNK), lambda b, i: (b, 0, i, 0)),
    ]
    out_shape = [
        jax.ShapeDtypeStruct((B, S, HEADS * HEAD_PAD), BF16),
        jax.ShapeDtypeStruct((B, S, HEADS * HEAD_PAD), BF16),
        jax.ShapeDtypeStruct((B, S, HEADS * HEAD_PAD), BF16),
        jax.ShapeDtypeStruct((B, n_ssm // SSM_GROUP, S // CHUNK * SSM_GROUP, CHUNK), F32),
    ]
    return pl.pallas_call(
        _front_kernel, grid=grid, in_specs=in_specs, out_specs=out_specs, out_shape=out_shape,
        compiler_params=_params("parallel", "parallel"), name="front",
    )(h, pos3, freq, gmix, wcat, wut, qg, wqm, wqs, kvg, wk, wv)


def _mla_kernel(qi_tab, kj_tab, bias_tab, q_ref, k_ref, v_ref, o_ref, bias_ref, *, tq, tk, n_steps, unroll):
    n = n_steps
    NH = q_ref.shape[2] // HEAD_PAD
    row = lax.broadcasted_iota(jnp.int32, (tq, tk), 0)
    col = lax.broadcasted_iota(jnp.int32, (tq, tk), 1)
    bias_ref[0] = jnp.zeros((tq, tk), F32)
    for d in range(tq // tk):
        bias_ref[1 + d] = jnp.where(col + d * tk <= row, 0.0, NEG).astype(F32)
    lane = lax.broadcasted_iota(jnp.int32, (tq, HEAD_PAD), 1)

    def rows(ref, r, size, hh):
        return ref[0, pl.ds(pl.multiple_of(r * size, size), size), hh * HEAD_PAD:(hh + 1) * HEAD_PAD]

    def scores(t):
        bias = bias_ref[bias_tab[t]]
        return [_dot_nt(rows(q_ref, qi_tab[t], tq, hh), rows(k_ref, kj_tab[t], tk, hh)) + bias for hh in range(NH)]

    def finish(t, accs, ps):
        accs = [accs[hh] + _dot(ps[hh], rows(v_ref, kj_tab[t], tk, hh)) for hh in range(NH)]
        o = [accs[hh] / accs[hh][:, V_SUM_LANE[hh % 2]:V_SUM_LANE[hh % 2] + 1] for hh in range(NH)]
        start = pl.multiple_of(qi_tab[t] * tq, tq)
        for pr in range(NH // 2):
            o_ref[0, pl.ds(start, tq), pr * HEAD_PAD:(pr + 1) * HEAD_PAD] = (
                jnp.where(lane < VDIM, o[2 * pr], o[2 * pr + 1]).astype(o_ref.dtype))
        return accs

    def softmax(t, ss, ms, accs):
        fresh = kj_tab[t] == 0
        new_ms, new_accs, new_ps = [], [], []
        for hh in range(NH):
            m_in = jnp.where(fresh, NEG, ms[hh])
            m_new = jnp.maximum(m_in, jnp.max(ss[hh], axis=-1, keepdims=True))
            new_ps.append(jnp.exp2(ss[hh] - m_new).astype(BF16))
            new_ms.append(m_new)
            new_accs.append(jnp.exp2(m_in - m_new) * accs[hh])
        return new_ms, new_accs, new_ps

    def trip(t, state):
        ms, accs, ps = state
        ss = scores(t)
        accs = finish(t - 1, accs, ps)
        return softmax(t, ss, ms, accs)

    def trips(i, state):
        for u in range(unroll):
            state = trip(1 + i * unroll + u, state)
        return state

    zero = jnp.zeros((tq, HEAD_PAD), F32)
    neg = jnp.full((tq, 1), NEG, F32)
    state = softmax(0, scores(0), [neg] * NH, [zero] * NH)
    looped = (n - 1) // unroll
    state = lax.fori_loop(0, looped, trips, state)
    for t in range(1 + looped * unroll, n):
        state = trip(t, state)
    finish(n - 1, state[1], state[2])


def _mla(q, k, v, *, tq, tk, unroll):
    B, S, _ = q.shape
    per = tq // tk
    steps = [(i, j) for i in range(S // tq) for j in range((i + 1) * per)]
    tab = lambda f: jnp.asarray([f(i, j) for i, j in steps], jnp.int32)
    qi_tab, kj_tab = tab(lambda i, j: i), tab(lambda i, j: j)
    bias_tab = tab(lambda i, j: max(0, j - i * per + 1))
    pair = lambda width: pl.BlockSpec((1, S, width), lambda b, p, *_: (b, 0, p))
    grid_spec = pltpu.PrefetchScalarGridSpec(
        num_scalar_prefetch=3, grid=(B, HEADS // MLA_HEADS_PER_STEP),
        in_specs=[pair(MLA_HEADS_PER_STEP * HEAD_PAD)] * 3,
        out_specs=pair(MLA_HEADS_PER_STEP * VDIM),
        scratch_shapes=[pltpu.VMEM((1 + per, tq, tk), F32)])
    return pl.pallas_call(
        functools.partial(_mla_kernel, tq=tq, tk=tk, n_steps=len(steps), unroll=unroll), grid_spec=grid_spec,
        out_shape=jax.ShapeDtypeStruct((B, S, HEADS * VDIM), BF16),
        compiler_params=_params("parallel", "parallel"), name="mla",
    )(qi_tab, kj_tab, bias_tab, q, k, v)


def _cpow(zr, zi, e):
    mag = jnp.exp(e * zr)
    return mag * jnp.cos(e * zi), mag * jnp.sin(e * zi)


def _s5_kernel(u_ref, ldt_ref, lrr_ref, lir_ref, lrc_ref, lic_ref, bx_ref, by_ref, cx_ref, cy_ref,
               y_ref, tt_ref, wst_ref, wout_ref, ep_ref, km_ref, lhs_ref, *, nb, nk):
    P2 = 2 * SSM_STATE
    G = SSM_GROUP
    dt = jnp.exp(ldt_ref[0])
    lr = lrr_ref[0]
    li = lir_ref[0]
    zr = lr * dt
    zi = li * dt
    ar, ai = _cpow(zr, zi, 1.0)
    den = lr * lr + li * li
    cr = ((ar - 1.0) * lr + ai * li) / den
    ci = (ai * lr - (ar - 1.0) * li) / den
    bx = bx_ref[0]
    by = by_ref[0]
    bbx = cr * bx + ci * by
    bby = cr * by - ci * bx
    lane = lax.broadcasted_iota(jnp.int32, (1, P2), 1)
    sign_lane = jnp.where(lane < SSM_STATE, 1.0, -1.0).astype(F32)
    e_rows = (CHUNK - 1 - lax.broadcasted_iota(jnp.int32, (CHUNK, 1), 0)).astype(F32)
    pr, pi = _cpow(zr, zi, e_rows)
    for c in range(G):
        wst_ref[c * CHUNK:(c + 1) * CHUNK, :] = (pr * bbx[c:c + 1, :] + pi * bby[c:c + 1, :]).astype(BF16)
    zrc = lrc_ref[0] * dt
    zic = lic_ref[0] * dt
    tau = lax.broadcasted_iota(jnp.int32, (1, CHUNK), 1).astype(F32)
    h0r, h0i = _cpow(zrc[:SSM_STATE], zic[:SSM_STATE], tau)
    a0r = jnp.concatenate([h0r, h0r], axis=0)
    a0i = jnp.concatenate([h0i, h0i], axis=0)
    acr, aci = _cpow(zrc, zic, 1.0)
    a1r, a1i = a0r * acr - a0i * aci, a0r * aci + a0i * acr
    cx = cx_ref[0]
    cy = cy_ref[0]
    sub = lax.broadcasted_iota(jnp.int32, (P2, 1), 0)
    sign_sub = jnp.where(sub < SSM_STATE, 1.0, -1.0).astype(F32)
    for c in range(G):
        cxc = cx[:, c:c + 1]
        cyc = cy[:, c:c + 1]
        ep_ref[:, c * CHUNK:(c + 1) * CHUNK] = a0r * cxc + a0i * cyc
        wout_ref[:, c * CHUNK:(c + 1) * CHUNK] = ((a1r * cxc + a1i * cyc) * sign_sub).astype(BF16)
    km_ref[...] = jnp.dot(bbx * sign_lane, ep_ref[...], preferred_element_type=F32,
                          precision=lax.Precision.HIGHEST)
    jj = lax.broadcasted_iota(jnp.int32, (CHUNK, CHUNK), 0)
    tcol = lax.broadcasted_iota(jnp.int32, (CHUNK, CHUNK), 1)
    lower = tcol >= jj

    def toeplitz_rows(cp, _):
        krow = km_ref[pl.ds(cp, 1), :]
        r0 = pl.multiple_of(cp * CHUNK, CHUNK)
        for c in range(G):
            blk = jnp.broadcast_to(krow[:, c * CHUNK:(c + 1) * CHUNK], (CHUNK, CHUNK))
            blk = pltpu.roll(blk, 0, 1, stride=1, stride_axis=0)
            tt_ref[pl.ds(r0, CHUNK), c * CHUNK:(c + 1) * CHUNK] = jnp.where(lower, blk, 0.0).astype(BF16)
        return 0

    lax.fori_loop(0, G, toeplitz_rows, 0)
    for b in range(nb):
        for c in range(G):
            lhs_ref[b * nk:(b + 1) * nk, c * CHUNK:(c + 1) * CHUNK] = (
                u_ref[b, 0, pl.ds(c, nk, stride=G), :].astype(BF16))
    lhs = lhs_ref[...]
    y = _dot(lhs, tt_ref[...])
    x = _dot(lhs, wst_ref[...])
    kidx = lax.broadcasted_iota(jnp.int32, (nb * nk, 1), 0) % nk
    qr, qi = ar, ai
    for _ in range(int(math.log2(CHUNK))):
        qr, qi = qr * qr - qi * qi, 2.0 * qr * qi
    d = 1
    while d < nk:
        sx = jnp.where(kidx >= d, pltpu.roll(x, d, 0), 0.0)
        x = x + qr * sx + (qi * -sign_lane) * pltpu.roll(sx, SSM_STATE, 1)
        qr, qi = qr * qr - qi * qi, 2.0 * qr * qi
        d *= 2
    x0 = jnp.where(kidx >= 1, pltpu.roll(x, 1, 0), 0.0)
    y = y + _dot(x0.astype(BF16), wout_ref[...])
    for b in range(nb):
        for c in range(G):
            y_ref[b, 0, pl.ds(c, nk, stride=G), :] = y[b * nk:(b + 1) * nk, c * CHUNK:(c + 1) * CHUNK]


def _s5(u4, ldt, lrr, lir, lrc, lic, bx, by, cx, cy, *, l):
    B, NG, R, _ = u4.shape
    nk = R // SSM_GROUP
    gspec = lambda arr: pl.BlockSpec((None, 1) + arr.shape[2:], lambda g: (l, g, 0, 0))
    ublk = pl.BlockSpec((B, 1, R, CHUNK), lambda g: (0, g, 0, 0))
    W = SSM_GROUP * CHUNK
    return pl.pallas_call(
        functools.partial(_s5_kernel, nb=B, nk=nk), grid=(NG,),
        in_specs=[ublk] + [gspec(a) for a in (ldt, lrr, lir, lrc, lic, bx, by, cx, cy)],
        out_specs=ublk,
        out_shape=jax.ShapeDtypeStruct(u4.shape, F32),
        scratch_shapes=[
            pltpu.VMEM((W, W), BF16),
            pltpu.VMEM((W, 2 * SSM_STATE), BF16),
            pltpu.VMEM((2 * SSM_STATE, W), BF16),
            pltpu.VMEM((2 * SSM_STATE, W), F32),
            pltpu.VMEM((SSM_GROUP, W), F32),
            pltpu.VMEM((B * nk, W), BF16),
        ],
        compiler_params=_params("parallel"), name="s5",
    )(u4, ldt, lrr, lir, lrc, lic, bx, by, cx, cy)


def _memkv_kernel(mem_ref, g_ref, w_ref, k_ref, v_ref):
    mn = _rms_rows(mem_ref[0], g_ref[...]).astype(BF16)
    kv = _dot(mn, w_ref[...])
    d = k_ref.shape[-1]
    k_ref[0] = kv[:, :d].astype(BF16)
    v_ref[0] = kv[:, d:].astype(BF16)


def _memkv(mem, g, w, *, l):
    B, M, D = mem.shape
    blk = pl.BlockSpec((1, M, D), lambda b: (b, 0, 0))
    return pl.pallas_call(
        _memkv_kernel, grid=(B,),
        in_specs=[blk, _layer_spec(g, l), _layer_spec(w, l)],
        out_specs=[blk, blk],
        out_shape=[jax.ShapeDtypeStruct((B, M, D), BF16)] * 2,
        compiler_params=_params("parallel"), name="memkv",
    )(mem, g, w)


def _mix_kernel(h_ref, a_ref, yt_ref, ut_ref, d_ref, wglut_ref, bglu_ref, ag_ref, sg_ref, woa_ref, wos_ref,
                xg_ref, wxq_ref, kx_ref, vx_ref, wxo_ref, o_ref):
    def channel_major(ref):
        n_ch = ref.shape[1] * SSM_GROUP
        return jnp.concatenate([ref[0, :, dk * SSM_GROUP:(dk + 1) * SSM_GROUP, :].reshape(n_ch, CHUNK)
                                for dk in range(ref.shape[2] // SSM_GROUP)], axis=1)

    y = channel_major(yt_ref) + d_ref[...] * channel_major(ut_ref)
    g = jax.nn.gelu(y)
    z = _dot(wglut_ref[...], g.astype(BF16)) + bglu_ref[...]
    s = y * jax.nn.sigmoid(z)
    s_n = s * lax.rsqrt(jnp.mean(s * s, axis=0, keepdims=True) + EPS) * sg_ref[...]
    a_n = _rms_rows(a_ref[0].astype(F32), ag_ref[...])
    h1 = h_ref[0] + _dot(a_n.astype(BF16), woa_ref[...]) + _dot(s_n.T.astype(BF16), wos_ref[...])
    qx = _dot(_rms_rows(h1, xg_ref[...]).astype(BF16), wxq_ref[...])
    dh = qx.shape[-1] // X_HEADS
    outs = []
    for hd in range(X_HEADS):
        sl = slice(hd * dh, (hd + 1) * dh)
        sc = _dot_nt(qx[:, sl].astype(BF16), kx_ref[0, :, sl]) * (dh ** -0.5)
        p = jnp.exp(sc - jnp.max(sc, axis=-1, keepdims=True))
        p = p / jnp.sum(p, axis=-1, keepdims=True)
        outs.append(_dot(p.astype(BF16), vx_ref[0, :, sl]))
    o = jnp.concatenate(outs, axis=-1)
    o_ref[0] = h1 + _dot(o.astype(BF16), wxo_ref[...])


def _mix(h, a_out, yt, ut, d, wglut, bglu, ag, sg, woa, wos, xg, wxq, kx, vx, wxo, *, ts, l):
    B, S, D = h.shape
    M = kx.shape[1]
    row = lambda b, i: (b, i, 0)
    colmaj = pl.BlockSpec((1, yt.shape[1], ts // CHUNK * SSM_GROUP, CHUNK), lambda b, i: (b, 0, i, 0))
    mem = pl.BlockSpec((1, M, D), lambda b, i: (b, 0, 0))
    consts = [d, wglut, bglu, ag, sg, woa, wos, xg, wxq]
    return pl.pallas_call(
        _mix_kernel, grid=(B, S // ts),
        in_specs=[pl.BlockSpec((1, ts, D), row), pl.BlockSpec((1, ts, a_out.shape[-1]), row), colmaj, colmaj]
                 + [_layer_spec(c, l) for c in consts] + [mem, mem, _layer_spec(wxo, l)],
        out_specs=pl.BlockSpec((1, ts, D), row),
        out_shape=jax.ShapeDtypeStruct((B, S, D), F32),
        compiler_params=_params("parallel", "parallel"), name="mix",
    )(h, a_out, yt, ut, *consts, kx, vx, wxo)


def _swiglu_kernel(h_ref, g_ref, wg_ref, wu_ref, wd_ref, fg_ref, o_ref, *, final_norm, chunks):
    h = h_ref[...]
    hn = _rms_rows(h, g_ref[...]).astype(BF16)
    out = h
    lo = 0
    for width in chunks:
        gate = _dot(hn, wg_ref[:, lo:lo + width])
        up = _dot(hn, wu_ref[:, lo:lo + width])
        out = out + _dot((gate * jax.nn.sigmoid(gate) * up).astype(BF16), wd_ref[lo:lo + width, :])
        lo += width
    if final_norm:
        out = _rms_rows(out, fg_ref[...])
    o_ref[...] = out


def _swiglu(h2d, g, wg, wu, wd, fg, *, tm, chunks, final_norm, l):
    T, D = h2d.shape
    return pl.pallas_call(
        functools.partial(_swiglu_kernel, final_norm=final_norm, chunks=chunks), grid=(T // tm,),
        in_specs=[pl.BlockSpec((tm, D), lambda i: (i, 0)), _layer_spec(g, l),
                  _layer_spec(wg, l, resident=True), _layer_spec(wu, l, resident=True),
                  _layer_spec(wd, l, resident=True), _const_spec(fg.shape)],
        out_specs=pl.BlockSpec((tm, D), lambda i: (i, 0)),
        out_shape=jax.ShapeDtypeStruct((T, D), F32),
        compiler_params=_params("parallel"), name="swiglu",
    )(h2d, g, wg, wu, wd, fg)


def _rope_swap(w):
    half = w.shape[-1] // 2
    return jnp.concatenate([-w[..., half:], w[..., :half]], axis=-1)


def _tile_of(d, prefs):
    for t in prefs:
        if d % t == 0:
            return t
    return d


def _split(total, target, align):
    n = -(-total // target)
    units = -(-total // align)
    widths = [(units // n + (1 if i < units % n else 0)) * align for i in range(n)]
    widths[-1] -= sum(widths) - total
    return tuple(w for w in widths if w > 0)


def kernel(x, mem, positions, norm_mix_g, w_in, q_norm_g, w_uq, kv_norm_g, w_ukv, ssm_lambda_re, ssm_lambda_im,
           ssm_log_dt, ssm_b_re, ssm_b_im, ssm_c_re, ssm_c_im, ssm_d, ssm_w_glu, ssm_b_glu, attn_out_g, ssm_out_g,
           w_out, norm_x_g, mem_norm_g, w_xq, w_xkv, w_xo, norm_ffn_g, w_gate, w_up, w_down, final_norm_g):
    B, S, D = x.shape
    depth = w_in.shape[0]
    n_ssm = ssm_d.shape[-1]
    n_groups = n_ssm // SSM_GROUP
    mla_w = HEADS * VDIM
    assert S % CHUNK == 0
    nk = S // CHUNK
    tm = _tile_of(S, (512, 256, 128))
    ff_chunks = _split(w_gate.shape[-1], 4 * MXU_WIDTH, MXU_WIDTH)

    pos3 = positions.reshape(B, S, 1)
    freqs = ROPE_THETA ** (-jnp.arange(0, ROPE, 2, dtype=F32) / ROPE)
    freq128 = jnp.zeros((1, LANES), F32).at[0, NOPE:NOPE + ROPE].set(jnp.concatenate([freqs, freqs]))
    zeros = lambda *s: jnp.zeros(s, F32)

    def layout(wi, wuq, wukv, lr, li, ldt, b_re, b_im, c_re, c_im, wglu, wo):
        w_kr = wi[:, Q_LORA + KV_LORA:Q_LORA + KV_LORA + ROPE]
        pad_rope = lambda w: jnp.concatenate([zeros(D, NOPE), w, zeros(D, HEAD_PAD - NOPE - ROPE)], axis=1)
        wcat = jnp.concatenate([wi[:, :Q_LORA + KV_LORA], pad_rope(w_kr), pad_rope(_rope_swap(w_kr))], axis=1).astype(BF16)
        wut = wi[:, Q_LORA + KV_LORA + ROPE:].T.astype(BF16)
        wq = wuq.reshape(Q_LORA, HEADS, NOPE + ROPE)
        zq = zeros(Q_LORA, HEADS, HEAD_PAD - NOPE - ROPE)
        wqm = jnp.concatenate([wq, zq], axis=-1).reshape(Q_LORA, HEADS * HEAD_PAD).astype(BF16)
        wqs = jnp.concatenate([zeros(Q_LORA, HEADS, NOPE), _rope_swap(wq[..., NOPE:]), zq], axis=-1)
        wqs = wqs.reshape(Q_LORA, HEADS * HEAD_PAD).astype(BF16)
        wkv = wukv.reshape(KV_LORA, HEADS, NOPE + VDIM)
        wk = jnp.concatenate([wkv[..., :NOPE], zeros(KV_LORA, HEADS, HEAD_PAD - NOPE)], axis=-1)
        wk = wk.reshape(KV_LORA, HEADS * HEAD_PAD).astype(BF16)
        wv2 = wkv[..., NOPE:].reshape(KV_LORA, HEADS // 2, 2, VDIM)
        zv = zeros(KV_LORA, HEADS // 2, HEAD_PAD - VDIM)
        wv = jnp.concatenate([wv2[:, :, 0], zv, zv, wv2[:, :, 1]], axis=-1)
        wv = wv.reshape(KV_LORA, HEADS * HEAD_PAD).astype(BF16)
        dup = lambda a: jnp.concatenate([a, a], axis=-1)
        brt, bit = jnp.swapaxes(b_re, 1, 2), jnp.swapaxes(b_im, 1, 2)
        crt, cit = jnp.swapaxes(c_re, 1, 2), jnp.swapaxes(c_im, 1, 2)
        s5p = (ldt.reshape(n_groups, 1, 1), dup(lr)[:, None, :], dup(li)[:, None, :], dup(lr)[:, :, None],
               dup(li)[:, :, None], jnp.concatenate([brt, bit], axis=-1), jnp.concatenate([-bit, brt], axis=-1),
               jnp.concatenate([crt, cit], axis=1), jnp.concatenate([-cit, crt], axis=1))
        return (wcat, wut, wqm, wqs, wk, wv), s5p, (wglu.T.astype(BF16), wo[:mla_w].astype(BF16),
                                                     wo[mla_w:].astype(BF16))

    front_w, s5_p, (wglut, woa, wos) = jax.vmap(layout)(
        w_in, w_uq, w_ukv, ssm_lambda_re, ssm_lambda_im, ssm_log_dt, ssm_b_re, ssm_b_im, ssm_c_re, ssm_c_im,
        ssm_w_glu, w_out)
    wcat, wut, wqm, wqs, wk, wv = front_w
    row = lambda g: g[:, None, :]
    col = lambda g: g[:, :, None]
    bf = lambda w: w.astype(BF16)
    wxq, wxkv, wxo, wg, wu, wd = bf(w_xq), bf(w_xkv), bf(w_xo), bf(w_gate), bf(w_up), bf(w_down)
    t_ff = _tile_of(B * S, (1024, 512, 256, 128))

    h = x
    for l in range(depth):
        q, k, v, ut = _front(h, pos3, freq128, row(norm_mix_g), wcat, wut, row(q_norm_g), wqm, wqs,
                             row(kv_norm_g), wk, wv, tm=tm, l=l)
        a_out = _mla(q, k, v, tq=tm, tk=min(MLA_TK, tm), unroll=MLA_UNROLL)
        yt = _s5(ut, *s5_p, l=l)
        kx, vx = _memkv(mem, row(mem_norm_g), wxkv, l=l)
        h = _mix(h, a_out, yt, ut, col(ssm_d), wglut, col(ssm_b_glu), row(attn_out_g), col(ssm_out_g), woa, wos,
                 row(norm_x_g), wxq, kx, vx, wxo, ts=tm, l=l)
        h = _swiglu(h.reshape(B * S, D), row(norm_ffn_g), wg, wu, wd, final_norm_g[None], tm=t_ff,
                    chunks=ff_chunks, final_norm=(l == depth - 1), l=l).reshape(B, S, D)
    return h
```

```python
import functools
import math

import jax
import jax.numpy as jnp
from jax import lax
from jax.experimental import pallas as pl
from jax.experimental.pallas import tpu as pltpu

LANES = 128
MXU_WIDTH = 256
VMEM_LIMIT_BYTES = 56 * 1024 * 1024

EPS = 1e-6
HEADS = 8
NOPE = 64
ROPE = 32
VDIM = 64
HEAD_PAD = LANES
V_SUM_LANE = (VDIM, 0)
Q_LORA = 256
KV_LORA = 128
ROPE_THETA = 10000.0
SSM_GROUP = 16
SSM_STATE = 64
CHUNK = LANES
X_HEADS = 4
MLA_TK = 512
MLA_UNROLL = 7
MLA_HEADS_PER_STEP = 2

F32 = jnp.float32
BF16 = jnp.bfloat16
NEG = -1e30


def _dot(a, b):
    return jnp.dot(a, b, preferred_element_type=F32)


def _dot_nt(a, b):
    return lax.dot_general(a, b, (((1,), (1,)), ((), ())), preferred_element_type=F32)


def _rms_rows(x, g):
    return x * lax.rsqrt(jnp.mean(x * x, axis=-1, keepdims=True) + EPS) * g


def _params(*sem):
    return pltpu.CompilerParams(dimension_semantics=sem, vmem_limit_bytes=VMEM_LIMIT_BYTES)


def _layer_spec(arr, l, resident=False):
    nd = arr.ndim
    return pl.BlockSpec((None,) + arr.shape[1:], lambda *_: (l,) + (0,) * (nd - 1),
                        pipeline_mode=pl.Buffered(1) if resident else None)


def _const_spec(shape):
    nd = len(shape)
    return pl.BlockSpec(shape, lambda *_: (0,) * nd)


def _front_kernel(h_ref, pos_ref, freq_ref, gmix_ref, wcat_ref, wut_ref, qg_ref, wqm_ref, wqs_ref,
                  kvg_ref, wk_ref, wv_ref, q_ref, k_ref, v_ref, ut_ref):
    h = h_ref[0]
    xn = _rms_rows(h, gmix_ref[...]).astype(BF16)
    proj = _dot(xn, wcat_ref[...])
    ut = _dot_nt(wut_ref[...], xn)
    n_groups = ut_ref.shape[1]
    for dk in range(ut.shape[1] // CHUNK):
        ut_ref[0, :, dk * SSM_GROUP:(dk + 1) * SSM_GROUP, :] = (
            ut[:, dk * CHUNK:(dk + 1) * CHUNK].reshape(n_groups, SSM_GROUP, CHUNK))
    cq = proj[:, :Q_LORA]
    ckv = proj[:, Q_LORA:Q_LORA + KV_LORA]
    kr_main = proj[:, Q_LORA + KV_LORA:Q_LORA + KV_LORA + LANES]
    kr_swap = proj[:, Q_LORA + KV_LORA + LANES:]
    ang = pos_ref[0].astype(F32) * freq_ref[...]
    cos = jnp.cos(ang)
    sin = jnp.sin(ang)
    scale = (NOPE + ROPE) ** -0.5 * math.log2(math.e)
    cqn = _rms_rows(cq, qg_ref[...]).astype(BF16)
    ckvn = _rms_rows(ckv, kvg_ref[...]).astype(BF16)
    qm = _dot(cqn, wqm_ref[...])
    qs = _dot(cqn, wqs_ref[...])
    cos_q = jnp.tile(cos * scale, (1, HEADS))
    sin_q = jnp.tile(sin * scale, (1, HEADS))
    q_ref[0] = (qm * cos_q + qs * sin_q).astype(BF16)
    k_rot = kr_main * cos + kr_swap * sin
    k_ref[0] = (_dot(ckvn, wk_ref[...]) + jnp.tile(k_rot, (1, HEADS))).astype(BF16)
    vlane = lax.broadcasted_iota(jnp.int32, (1, HEADS * HEAD_PAD), 1) % (2 * HEAD_PAD)
    ones = jnp.where((vlane == V_SUM_LANE[0]) | (vlane == HEAD_PAD + V_SUM_LANE[1]), 1.0, 0.0).astype(F32)
    v_ref[0] = (_dot(ckvn, wv_ref[...]) + ones).astype(BF16)


def _front(h, pos3, freq, gmix, wcat, wut, qg, wqm, wqs, kvg, wk, wv, *, tm, l):
    B, S, D = h.shape
    n_ssm = wut.shape[1]
    grid = (B, S // tm)
    row = lambda b, i: (b, i, 0)
    in_specs = [
        pl.BlockSpec((1, tm, D), row),
        pl.BlockSpec((1, tm, 1), row),
        _const_spec(freq.shape),
    ] + [_layer_spec(w, l) for w in (gmix, wcat, wut, qg, wqm, wqs, kvg, wk, wv)]
    out_specs = [
        pl.BlockSpec((1, tm, HEADS * HEAD_PAD), row),
        pl.BlockSpec((1, tm, HEADS * HEAD_PAD), row),
        pl.BlockSpec((1, tm, HEADS * HEAD_PAD), row),
        pl.BlockSpec((1, n_ssm // SSM_GROUP, tm // CHUNK * SSM_GROUP, CHUNK), lambda b, i: (b, 0, i, 0)),
    ]
    out_shape = [
        jax.ShapeDtypeStruct((B, S, HEADS * HEAD_PAD), BF16),
        jax.ShapeDtypeStruct((B, S, HEADS * HEAD_PAD), BF16),
        jax.ShapeDtypeStruct((B, S, HEADS * HEAD_PAD), BF16),
        jax.ShapeDtypeStruct((B, n_ssm // SSM_GROUP, S // CHUNK * SSM_GROUP, CHUNK), F32),
    ]
    return pl.pallas_call(
        _front_kernel, grid=grid, in_specs=in_specs, out_specs=out_specs, out_shape=out_shape,
        compiler_params=_params("parallel", "parallel"), name="front",
    )(h, pos3, freq, gmix, wcat, wut, qg, wqm, wqs, kvg, wk, wv)


def _mla_kernel(qi_tab, kj_tab, bias_tab, q_ref, k_ref, v_ref, o_ref, bias_ref, *, tq, tk, n_steps, unroll):
    n = n_steps
    NH = q_ref.shape[2] // HEAD_PAD
    row = lax.broadcasted_iota(jnp.int32, (tq, tk), 0)
    col = lax.broadcasted_iota(jnp.int32, (tq, tk), 1)
    bias_ref[0] = jnp.zeros((tq, tk), F32)
    for d in range(tq // tk):
        bias_ref[1 + d] = jnp.where(col + d * tk <= row, 0.0, NEG).astype(F32)
    lane = lax.broadcasted_iota(jnp.int32, (tq, HEAD_PAD), 1)

    def rows(ref, r, size, hh):
        return ref[0, pl.ds(pl.multiple_of(r * size, size), size), hh * HEAD_PAD:(hh + 1) * HEAD_PAD]

    def scores(t):
        bias = bias_ref[bias_tab[t]]
        return [_dot_nt(rows(q_ref, qi_tab[t], tq, hh), rows(k_ref, kj_tab[t], tk, hh)) + bias for hh in range(NH)]

    def weigh(t, accs, ps):
        return [accs[hh] + _dot(ps[hh], rows(v_ref, kj_tab[t], tk, hh)) for hh in range(NH)]

    def emit(t, accs):
        o = [accs[hh] / accs[hh][:, V_SUM_LANE[hh % 2]:V_SUM_LANE[hh % 2] + 1] for hh in range(NH)]
        start = pl.multiple_of(qi_tab[t] * tq, tq)
        for pr in range(NH // 2):
            o_ref[0, pl.ds(start, tq), pr * HEAD_PAD:(pr + 1) * HEAD_PAD] = (
                jnp.where(lane < VDIM, o[2 * pr], o[2 * pr + 1]).astype(o_ref.dtype))

    def softmax(t, ss, ms, accs):
        fresh = kj_tab[t] == 0
        new_ms, new_accs, new_ps = [], [], []
        for hh in range(NH):
            m_in = jnp.where(fresh, NEG, ms[hh])
            m_new = jnp.maximum(m_in, jnp.max(ss[hh], axis=-1, keepdims=True))
            new_ps.append(jnp.exp2(ss[hh] - m_new).astype(BF16))
            new_ms.append(m_new)
            new_accs.append(jnp.exp2(m_in - m_new) * accs[hh])
        return new_ms, new_accs, new_ps

    def trip(t, state):
        ms, accs, ps = state
        ss = scores(t)
        accs = weigh(t - 1, accs, ps)
        state = softmax(t, ss, ms, accs)
        emit(t - 1, accs)
        return state

    def trips(i, state):
        for u in range(unroll):
            state = trip(1 + i * unroll + u, state)
        return state

    zero = jnp.zeros((tq, HEAD_PAD), F32)
    neg = jnp.full((tq, 1), NEG, F32)
    state = softmax(0, scores(0), [neg] * NH, [zero] * NH)
    looped = (n - 1) // unroll
    state = lax.fori_loop(0, looped, trips, state)
    for t in range(1 + looped * unroll, n):
        state = trip(t, state)
    emit(n - 1, weigh(n - 1, state[1], state[2]))


def _mla(q, k, v, *, tq, tk, unroll):
    B, S, _ = q.shape
    per = tq // tk
    steps = [(i, j) for i in range(S // tq) for j in range((i + 1) * per)]
    tab = lambda f: jnp.asarray([f(i, j) for i, j in steps], jnp.int32)
    qi_tab, kj_tab = tab(lambda i, j: i), tab(lambda i, j: j)
    bias_tab = tab(lambda i, j: max(0, j - i * per + 1))
    pair = lambda width: pl.BlockSpec((1, S, width), lambda b, p, *_: (b, 0, p))
    grid_spec = pltpu.PrefetchScalarGridSpec(
        num_scalar_prefetch=3, grid=(B, HEADS // MLA_HEADS_PER_STEP),
        in_specs=[pair(MLA_HEADS_PER_STEP * HEAD_PAD)] * 3,
        out_specs=pair(MLA_HEADS_PER_STEP * VDIM),
        scratch_shapes=[pltpu.VMEM((1 + per, tq, tk), F32)])
    return pl.pallas_call(
        functools.partial(_mla_kernel, tq=tq, tk=tk, n_steps=len(steps), unroll=unroll), grid_spec=grid_spec,
        out_shape=jax.ShapeDtypeStruct((B, S, HEADS * VDIM), BF16),
        compiler_params=_params("parallel", "parallel"), name="mla",
    )(qi_tab, kj_tab, bias_tab, q, k, v)


def _cpow(zr, zi, e):
    mag = jnp.exp(e * zr)
    return mag * jnp.cos(e * zi), mag * jnp.sin(e * zi)


def _s5_kernel(u_ref, ldt_ref, lrr_ref, lir_ref, lrc_ref, lic_ref, bx_ref, by_ref, cx_ref, cy_ref,
               y_ref, tt_ref, wst_ref, wout_ref, ep_ref, km_ref, lhs_ref, *, nb, nk):
    P2 = 2 * SSM_STATE
    G = SSM_GROUP
    dt = jnp.exp(ldt_ref[0])
    lr = lrr_ref[0]
    li = lir_ref[0]
    zr = lr * dt
    zi = li * dt
    ar, ai = _cpow(zr, zi, 1.0)
    den = lr * lr + li * li
    cr = ((ar - 1.0) * lr + ai * li) / den
    ci = (ai * lr - (ar - 1.0) * li) / den
    bx = bx_ref[0]
    by = by_ref[0]
    bbx = cr * bx + ci * by
    bby = cr * by - ci * bx
    lane = lax.broadcasted_iota(jnp.int32, (1, P2), 1)
    sign_lane = jnp.where(lane < SSM_STATE, 1.0, -1.0).astype(F32)
    e_rows = (CHUNK - 1 - lax.broadcasted_iota(jnp.int32, (CHUNK, 1), 0)).astype(F32)
    pr, pi = _cpow(zr, zi, e_rows)
    for c in range(G):
        wst_ref[c * CHUNK:(c + 1) * CHUNK, :] = (pr * bbx[c:c + 1, :] + pi * bby[c:c + 1, :]).astype(BF16)
    zrc = lrc_ref[0] * dt
    zic = lic_ref[0] * dt
    tau = lax.broadcasted_iota(jnp.int32, (1, CHUNK), 1).astype(F32)
    h0r, h0i = _cpow(zrc[:SSM_STATE], zic[:SSM_STATE], tau)
    a0r = jnp.concatenate([h0r, h0r], axis=0)
    a0i = jnp.concatenate([h0i, h0i], axis=0)
    acr, aci = _cpow(zrc, zic, 1.0)
    a1r, a1i = a0r * acr - a0i * aci, a0r * aci + a0i * acr
    cx = cx_ref[0]
    cy = cy_ref[0]
    sub = lax.broadcasted_iota(jnp.int32, (P2, 1), 0)
    sign_sub = jnp.where(sub < SSM_STATE, 1.0, -1.0).astype(F32)
    for c in range(G):
        cxc = cx[:, c:c + 1]
        cyc = cy[:, c:c + 1]
        ep_ref[:, c * CHUNK:(c + 1) * CHUNK] = a0r * cxc + a0i * cyc
        wout_ref[:, c * CHUNK:(c + 1) * CHUNK] = ((a1r * cxc + a1i * cyc) * sign_sub).astype(BF16)
    km_ref[...] = jnp.dot(bbx * sign_lane, ep_ref[...], preferred_element_type=F32,
                          precision=lax.Precision.HIGHEST)
    jj = lax.broadcasted_iota(jnp.int32, (CHUNK, CHUNK), 0)
    tcol = lax.broadcasted_iota(jnp.int32, (CHUNK, CHUNK), 1)
    lower = tcol >= jj

    def input_channels(i, _):
        for cp in (2 * i, 2 * i + 1):
            krow = km_ref[pl.ds(cp, 1), :]
            r0 = pl.multiple_of(cp * CHUNK, CHUNK)
            for c in range(G):
                blk = jnp.broadcast_to(krow[:, c * CHUNK:(c + 1) * CHUNK], (CHUNK, CHUNK))
                blk = pltpu.roll(blk, 0, 1, stride=1, stride_axis=0)
                tt_ref[pl.ds(r0, CHUNK), c * CHUNK:(c + 1) * CHUNK] = jnp.where(lower, blk, 0.0).astype(BF16)
            for b in range(nb):
                lhs_ref[b * nk:(b + 1) * nk, pl.ds(r0, CHUNK)] = u_ref[b, 0, pl.ds(cp, nk, stride=G), :].astype(BF16)
        return 0

    lax.fori_loop(0, G // 2, input_channels, 0)
    lhs = lhs_ref[...]
    y = _dot(lhs, tt_ref[...])
    x = _dot(lhs, wst_ref[...])
    kidx = lax.broadcasted_iota(jnp.int32, (nb * nk, 1), 0) % nk
    qr, qi = ar, ai
    for _ in range(int(math.log2(CHUNK))):
        qr, qi = qr * qr - qi * qi, 2.0 * qr * qi
    d = 1
    while d < nk:
        sx = jnp.where(kidx >= d, pltpu.roll(x, d, 0), 0.0)
        x = x + qr * sx + (qi * -sign_lane) * pltpu.roll(sx, SSM_STATE, 1)
        qr, qi = qr * qr - qi * qi, 2.0 * qr * qi
        d *= 2
    x0 = jnp.where(kidx >= 1, pltpu.roll(x, 1, 0), 0.0)
    y = y + _dot(x0.astype(BF16), wout_ref[...])
    for b in range(nb):
        for c in range(G):
            y_ref[b, 0, pl.ds(c, nk, stride=G), :] = y[b * nk:(b + 1) * nk, c * CHUNK:(c + 1) * CHUNK]


def _s5(u4, ldt, lrr, lir, lrc, lic, bx, by, cx, cy, *, l):
    B, NG, R, _ = u4.shape
    nk = R // SSM_GROUP
    gspec = lambda arr: pl.BlockSpec((None, 1) + arr.shape[2:], lambda g: (l, g, 0, 0))
    ublk = pl.BlockSpec((B, 1, R, CHUNK), lambda g: (0, g, 0, 0))
    W = SSM_GROUP * CHUNK
    return pl.pallas_call(
        functools.partial(_s5_kernel, nb=B, nk=nk), grid=(NG,),
        in_specs=[ublk] + [gspec(a) for a in (ldt, lrr, lir, lrc, lic, bx, by, cx, cy)],
        out_specs=ublk,
        out_shape=jax.ShapeDtypeStruct(u4.shape, F32),
        scratch_shapes=[
            pltpu.VMEM((W, W), BF16),
            pltpu.VMEM((W, 2 * SSM_STATE), BF16),
            pltpu.VMEM((2 * SSM_STATE, W), BF16),
            pltpu.VMEM((2 * SSM_STATE, W), F32),
            pltpu.VMEM((SSM_GROUP, W), F32),
            pltpu.VMEM((B * nk, W), BF16),
        ],
        compiler_params=_params("parallel"), name="s5",
    )(u4, ldt, lrr, lir, lrc, lic, bx, by, cx, cy)


def _memkv_kernel(mem_ref, g_ref, w_ref, k_ref, v_ref):
    mn = _rms_rows(mem_ref[0], g_ref[...]).astype(BF16)
    kv = _dot(mn, w_ref[...])
    d = k_ref.shape[-1]
    k_ref[0] = kv[:, :d].astype(BF16)
    v_ref[0] = kv[:, d:].astype(BF16)


def _memkv(mem, g, w, *, l):
    B, M, D = mem.shape
    blk = pl.BlockSpec((1, M, D), lambda b: (b, 0, 0))
    return pl.pallas_call(
        _memkv_kernel, grid=(B,),
        in_specs=[blk, _layer_spec(g, l), _layer_spec(w, l)],
        out_specs=[blk, blk],
        out_shape=[jax.ShapeDtypeStruct((B, M, D), BF16)] * 2,
        compiler_params=_params("parallel"), name="memkv",
    )(mem, g, w)


def _mix_kernel(h_ref, a_ref, yt_ref, ut_ref, d_ref, wglut_ref, bglu_ref, ag_ref, sg_ref, woa_ref, wos_ref,
                xg_ref, wxq_ref, kx_ref, vx_ref, wxo_ref, o_ref):
    def channel_major(ref):
        n_ch = ref.shape[1] * SSM_GROUP
        return jnp.concatenate([ref[0, :, dk * SSM_GROUP:(dk + 1) * SSM_GROUP, :].reshape(n_ch, CHUNK)
                                for dk in range(ref.shape[2] // SSM_GROUP)], axis=1)

    y = channel_major(yt_ref) + d_ref[...] * channel_major(ut_ref)
    g = jax.nn.gelu(y)
    z = _dot(wglut_ref[...], g.astype(BF16)) + bglu_ref[...]
    s = y * jax.nn.sigmoid(z)
    s_n = s * lax.rsqrt(jnp.mean(s * s, axis=0, keepdims=True) + EPS) * sg_ref[...]
    a_n = _rms_rows(a_ref[0].astype(F32), ag_ref[...])
    h1 = h_ref[0] + _dot(a_n.astype(BF16), woa_ref[...]) + _dot(s_n.T.astype(BF16), wos_ref[...])
    qx = _dot(_rms_rows(h1, xg_ref[...]).astype(BF16), wxq_ref[...])
    dh = qx.shape[-1] // X_HEADS
    outs = []
    for hd in range(X_HEADS):
        sl = slice(hd * dh, (hd + 1) * dh)
        sc = _dot_nt(qx[:, sl].astype(BF16), kx_ref[0, :, sl]) * (dh ** -0.5)
        p = jnp.exp(sc - jnp.max(sc, axis=-1, keepdims=True))
        p = p / jnp.sum(p, axis=-1, keepdims=True)
        outs.append(_dot(p.astype(BF16), vx_ref[0, :, sl]))
    o = jnp.concatenate(outs, axis=-1)
    o_ref[0] = h1 + _dot(o.astype(BF16), wxo_ref[...])


def _mix(h, a_out, yt, ut, d, wglut, bglu, ag, sg, woa, wos, xg, wxq, kx, vx, wxo, *, ts, l):
    B, S, D = h.shape
    M = kx.shape[1]
    row = lambda b, i: (b, i, 0)
    colmaj = pl.BlockSpec((1, yt.shape[1], ts // CHUNK * SSM_GROUP, CHUNK), lambda b, i: (b, 0, i, 0))
    mem = pl.BlockSpec((1, M, D), lambda b, i: (b, 0, 0))
    consts = [d, wglut, bglu, ag, sg, woa, wos, xg, wxq]
    return pl.pallas_call(
        _mix_kernel, grid=(B, S // ts),
        in_specs=[pl.BlockSpec((1, ts, D), row), pl.BlockSpec((1, ts, a_out.shape[-1]), row), colmaj, colmaj]
                 + [_layer_spec(c, l, resident=True) for c in consts] + [mem, mem, _layer_spec(wxo, l, resident=True)],
        out_specs=pl.BlockSpec((1, ts, D), row),
        out_shape=jax.ShapeDtypeStruct((B, S, D), F32),
        compiler_params=_params("parallel", "parallel"), name="mix",
    )(h, a_out, yt, ut, *consts, kx, vx, wxo)


def _swiglu_kernel(h_ref, g_ref, wg_ref, wu_ref, wd_ref, fg_ref, o_ref, *, final_norm, chunks):
    h = h_ref[...]
    hn = _rms_rows(h, g_ref[...]).astype(BF16)
    out = h
    lo = 0
    for width in chunks:
        gate = _dot(hn, wg_ref[:, lo:lo + width])
        up = _dot(hn, wu_ref[:, lo:lo + width])
        out = out + _dot((gate * jax.nn.sigmoid(gate) * up).astype(BF16), wd_ref[lo:lo + width, :])
        lo += width
    if final_norm:
        out = _rms_rows(out, fg_ref[...])
    o_ref[...] = out


def _swiglu(h2d, g, wg, wu, wd, fg, *, tm, chunks, final_norm, l):
    T, D = h2d.shape
    return pl.pallas_call(
        functools.partial(_swiglu_kernel, final_norm=final_norm, chunks=chunks), grid=(T // tm,),
        in_specs=[pl.BlockSpec((tm, D), lambda i: (i, 0)), _layer_spec(g, l),
                  _layer_spec(wg, l, resident=True), _layer_spec(wu, l, resident=True),
                  _layer_spec(wd, l, resident=True), _const_spec(fg.shape)],
        out_specs=pl.BlockSpec((tm, D), lambda i: (i, 0)),
        out_shape=jax.ShapeDtypeStruct((T, D), F32),
        compiler_params=_params("parallel"), name="swiglu",
    )(h2d, g, wg, wu, wd, fg)


def _rope_swap(w):
    half = w.shape[-1] // 2
    return jnp.concatenate([-w[..., half:], w[..., :half]], axis=-1)


def _tile_of(d, prefs):
    for t in prefs:
        if d % t == 0:
            return t
    return d


def _split(total, target, align):
    n = -(-total // target)
    units = -(-total // align)
    widths = [(units // n + (1 if i < units % n else 0)) * align for i in range(n)]
    widths[-1] -= sum(widths) - total
    return tuple(w for w in widths if w > 0)


def kernel(x, mem, positions, norm_mix_g, w_in, q_norm_g, w_uq, kv_norm_g, w_ukv, ssm_lambda_re, ssm_lambda_im,
           ssm_log_dt, ssm_b_re, ssm_b_im, ssm_c_re, ssm_c_im, ssm_d, ssm_w_glu, ssm_b_glu, attn_out_g, ssm_out_g,
           w_out, norm_x_g, mem_norm_g, w_xq, w_xkv, w_xo, norm_ffn_g, w_gate, w_up, w_down, final_norm_g):
    B, S, D = x.shape
    depth = w_in.shape[0]
    n_ssm = ssm_d.shape[-1]
    n_groups = n_ssm // SSM_GROUP
    mla_w = HEADS * VDIM
    assert S % CHUNK == 0
    nk = S // CHUNK
    tm = _tile_of(S, (512, 256, 128))
    ff_chunks = _split(w_gate.shape[-1], 4 * MXU_WIDTH, MXU_WIDTH)

    pos3 = positions.reshape(B, S, 1)
    freqs = ROPE_THETA ** (-jnp.arange(0, ROPE, 2, dtype=F32) / ROPE)
    freq128 = jnp.zeros((1, LANES), F32).at[0, NOPE:NOPE + ROPE].set(jnp.concatenate([freqs, freqs]))
    zeros = lambda *s: jnp.zeros(s, F32)

    def layout(wi, wuq, wukv, lr, li, ldt, b_re, b_im, c_re, c_im, wglu, wo):
        w_kr = wi[:, Q_LORA + KV_LORA:Q_LORA + KV_LORA + ROPE]
        pad_rope = lambda w: jnp.concatenate([zeros(D, NOPE), w, zeros(D, HEAD_PAD - NOPE - ROPE)], axis=1)
        wcat = jnp.concatenate([wi[:, :Q_LORA + KV_LORA], pad_rope(w_kr), pad_rope(_rope_swap(w_kr))], axis=1).astype(BF16)
        wut = wi[:, Q_LORA + KV_LORA + ROPE:].T.astype(BF16)
        wq = wuq.reshape(Q_LORA, HEADS, NOPE + ROPE)
        zq = zeros(Q_LORA, HEADS, HEAD_PAD - NOPE - ROPE)
        wqm = jnp.concatenate([wq, zq], axis=-1).reshape(Q_LORA, HEADS * HEAD_PAD).astype(BF16)
        wqs = jnp.concatenate([zeros(Q_LORA, HEADS, NOPE), _rope_swap(wq[..., NOPE:]), zq], axis=-1)
        wqs = wqs.reshape(Q_LORA, HEADS * HEAD_PAD).astype(BF16)
        wkv = wukv.reshape(KV_LORA, HEADS, NOPE + VDIM)
        wk = jnp.concatenate([wkv[..., :NOPE], zeros(KV_LORA, HEADS, HEAD_PAD - NOPE)], axis=-1)
        wk = wk.reshape(KV_LORA, HEADS * HEAD_PAD).astype(BF16)
        wv2 = wkv[..., NOPE:].reshape(KV_LORA, HEADS // 2, 2, VDIM)
        zv = zeros(KV_LORA, HEADS // 2, HEAD_PAD - VDIM)
        wv = jnp.concatenate([wv2[:, :, 0], zv, zv, wv2[:, :, 1]], axis=-1)
        wv = wv.reshape(KV_LORA, HEADS * HEAD_PAD).astype(BF16)
        dup = lambda a: jnp.concatenate([a, a], axis=-1)
        brt, bit = jnp.swapaxes(b_re, 1, 2), jnp.swapaxes(b_im, 1, 2)
        crt, cit = jnp.swapaxes(c_re, 1, 2), jnp.swapaxes(c_im, 1, 2)
        s5p = (ldt.reshape(n_groups, 1, 1), dup(lr)[:, None, :], dup(li)[:, None, :], dup(lr)[:, :, None],
               dup(li)[:, :, None], jnp.concatenate([brt, bit], axis=-1), jnp.concatenate([-bit, brt], axis=-1),
               jnp.concatenate([crt, cit], axis=1), jnp.concatenate([-cit, crt], axis=1))
        return (wcat, wut, wqm, wqs, wk, wv), s5p, (wglu.T.astype(BF16), wo[:mla_w].astype(BF16),
                                                     wo[mla_w:].astype(BF16))

    front_w, s5_p, (wglut, woa, wos) = jax.vmap(layout)(
        w_in, w_uq, w_ukv, ssm_lambda_re, ssm_lambda_im, ssm_log_dt, ssm_b_re, ssm_b_im, ssm_c_re, ssm_c_im,
        ssm_w_glu, w_out)
    wcat, wut, wqm, wqs, wk, wv = front_w
    row = lambda g: g[:, None, :]
    col = lambda g: g[:, :, None]
    bf = lambda w: w.astype(BF16)
    wxq, wxkv, wxo, wg, wu, wd = bf(w_xq), bf(w_xkv), bf(w_xo), bf(w_gate), bf(w_up), bf(w_down)
    t_ff = _tile_of(B * S, (1024, 512, 256, 128))

    h = x
    for l in range(depth):
        q, k, v, ut = _front(h, pos3, freq128, row(norm_mix_g), wcat, wut, row(q_norm_g), wqm, wqs,
                             row(kv_norm_g), wk, wv, tm=_tile_of(S, (1024, 512, 256, 128)), l=l)
        a_out = _mla(q, k, v, tq=tm, tk=min(MLA_TK, tm), unroll=MLA_UNROLL)
        yt = _s5(ut, *s5_p, l=l)
        kx, vx = _memkv(mem, row(mem_norm_g), wxkv, l=l)
        h = _mix(h, a_out, yt, ut, col(ssm_d), wglut, col(ssm_b_glu), row(attn_out_g), col(ssm_out_g), woa, wos,
                 row(norm_x_g), wxq, kx, vx, wxo, ts=_tile_of(S, (1024, 512, 256, 128)), l=l)
        h = _swiglu(h.reshape(B * S, D), row(norm_ffn_g), wg, wu, wd, final_norm_g[None], tm=t_ff,
                    chunks=ff_chunks, final_norm=(l == depth - 1), l=l).reshape(B, S, D)
    return h
```

```python
import functools
import math

import jax
import jax.numpy as jnp
from jax import lax
from jax.experimental import pallas as pl
from jax.experimental.pallas import tpu as pltpu

LANES = 128
MXU_WIDTH = 256
VMEM_LIMIT_BYTES = 56 * 1024 * 1024

EPS = 1e-6
HEADS = 8
NOPE = 64
ROPE = 32
VDIM = 64
HEAD_PAD = LANES
V_SUM_LANE = (VDIM, 0)
Q_LORA = 256
KV_LORA = 128
ROPE_THETA = 10000.0
SSM_GROUP = 16
SSM_STATE = 64
CHUNK = LANES
X_HEADS = 4
MLA_TK = 512
MLA_UNROLL = 7
MLA_HEADS_PER_STEP = 2

F32 = jnp.float32
BF16 = jnp.bfloat16
NEG = -1e30


def _dot(a, b):
    return jnp.dot(a, b, preferred_element_type=F32)


def _dot_nt(a, b):
    return lax.dot_general(a, b, (((1,), (1,)), ((), ())), preferred_element_type=F32)


def _rms_rows(x, g):
    return x * lax.rsqrt(jnp.mean(x * x, axis=-1, keepdims=True) + EPS) * g


def _params(*sem):
    return pltpu.CompilerParams(dimension_semantics=sem, vmem_limit_bytes=VMEM_LIMIT_BYTES)


def _layer_spec(arr, l, resident=False):
    nd = arr.ndim
    return pl.BlockSpec((None,) + arr.shape[1:], lambda *_: (l,) + (0,) * (nd - 1),
                        pipeline_mode=pl.Buffered(1) if resident else None)


def _const_spec(shape):
    nd = len(shape)
    return pl.BlockSpec(shape, lambda *_: (0,) * nd)


def _front_kernel(h_ref, pos_ref, freq_ref, gmix_ref, wcat_ref, wut_ref, qg_ref, wqm_ref, wqs_ref,
                  kvg_ref, wk_ref, wv_ref, q_ref, k_ref, v_ref, ut_ref):
    h = h_ref[0]
    xn = _rms_rows(h, gmix_ref[...]).astype(BF16)
    proj = _dot(xn, wcat_ref[...])
    ut = _dot_nt(wut_ref[...], xn)
    n_groups = ut_ref.shape[1]
    for dk in range(ut.shape[1] // CHUNK):
        ut_ref[0, :, dk * SSM_GROUP:(dk + 1) * SSM_GROUP, :] = (
            ut[:, dk * CHUNK:(dk + 1) * CHUNK].reshape(n_groups, SSM_GROUP, CHUNK))
    cq = proj[:, :Q_LORA]
    ckv = proj[:, Q_LORA:Q_LORA + KV_LORA]
    kr_main = proj[:, Q_LORA + KV_LORA:Q_LORA + KV_LORA + LANES]
    kr_swap = proj[:, Q_LORA + KV_LORA + LANES:]
    ang = pos_ref[0].astype(F32) * freq_ref[...]
    cos = jnp.cos(ang)
    sin = jnp.sin(ang)
    scale = (NOPE + ROPE) ** -0.5 * math.log2(math.e)
    cqn = _rms_rows(cq, qg_ref[...]).astype(BF16)
    ckvn = _rms_rows(ckv, kvg_ref[...]).astype(BF16)
    qm = _dot(cqn, wqm_ref[...])
    qs = _dot(cqn, wqs_ref[...])
    cos_q = jnp.tile(cos * scale, (1, HEADS))
    sin_q = jnp.tile(sin * scale, (1, HEADS))
    q_ref[0] = (qm * cos_q + qs * sin_q).astype(BF16)
    k_rot = kr_main * cos + kr_swap * sin
    k_ref[0] = (_dot(ckvn, wk_ref[...]) + jnp.tile(k_rot, (1, HEADS))).astype(BF16)
    vlane = lax.broadcasted_iota(jnp.int32, (1, HEADS * HEAD_PAD), 1) % (2 * HEAD_PAD)
    ones = jnp.where((vlane == V_SUM_LANE[0]) | (vlane == HEAD_PAD + V_SUM_LANE[1]), 1.0, 0.0).astype(F32)
    v_ref[0] = (_dot(ckvn, wv_ref[...]) + ones).astype(BF16)


def _front(h, pos3, freq, gmix, wcat, wut, qg, wqm, wqs, kvg, wk, wv, *, tm, l):
    B, S, D = h.shape
    n_ssm = wut.shape[1]
    grid = (B, S // tm)
    row = lambda b, i: (b, i, 0)
    in_specs = [
        pl.BlockSpec((1, tm, D), row),
        pl.BlockSpec((1, tm, 1), row),
        _const_spec(freq.shape),
    ] + [_layer_spec(w, l) for w in (gmix, wcat, wut, qg, wqm, wqs, kvg, wk, wv)]
    out_specs = [
        pl.BlockSpec((1, tm, HEADS * HEAD_PAD), row),
        pl.BlockSpec((1, tm, HEADS * HEAD_PAD), row),
        pl.BlockSpec((1, tm, HEADS * HEAD_PAD), row),
        pl.BlockSpec((1, n_ssm // SSM_GROUP, tm // CHUNK * SSM_GROUP, CHUNK), lambda b, i: (b, 0, i, 0)),
    ]
    out_shape = [
        jax.ShapeDtypeStruct((B, S, HEADS * HEAD_PAD), BF16),
        jax.ShapeDtypeStruct((B, S, HEADS * HEAD_PAD), BF16),
        jax.ShapeDtypeStruct((B, S, HEADS * HEAD_PAD), BF16),
        jax.ShapeDtypeStruct((B, n_ssm // SSM_GROUP, S // CHUNK * SSM_GROUP, CHUNK), F32),
    ]
    return pl.pallas_call(
        _front_kernel, grid=grid, in_specs=in_specs, out_specs=out_specs, out_shape=out_shape,
        compiler_params=_params("parallel", "parallel"), name="front",
    )(h, pos3, freq, gmix, wcat, wut, qg, wqm, wqs, kvg, wk, wv)


def _mla_kernel(qi_tab, kj_tab, bias_tab, q_ref, k_ref, v_ref, o_ref, bias_ref, *, tq, tk, n_steps, unroll):
    n = n_steps
    NH = q_ref.shape[2] // HEAD_PAD
    row = lax.broadcasted_iota(jnp.int32, (tq, tk), 0)
    col = lax.broadcasted_iota(jnp.int32, (tq, tk), 1)
    bias_ref[0] = jnp.zeros((tq, tk), F32)
    for d in range(tq // tk):
        bias_ref[1 + d] = jnp.where(col + d * tk <= row, 0.0, NEG).astype(F32)
    lane = lax.broadcasted_iota(jnp.int32, (tq, HEAD_PAD), 1)

    def rows(ref, r, size, hh):
        return ref[0, pl.ds(pl.multiple_of(r * size, size), size), hh * HEAD_PAD:(hh + 1) * HEAD_PAD]

    def scores(t):
        bias = bias_ref[bias_tab[t]]
        return [_dot_nt(rows(q_ref, qi_tab[t], tq, hh), rows(k_ref, kj_tab[t], tk, hh)) + bias for hh in range(NH)]

    def weigh(t, accs, ps):
        return [accs[hh] + _dot(ps[hh], rows(v_ref, kj_tab[t], tk, hh)) for hh in range(NH)]

    def emit(t, accs):
        o = [accs[hh] / accs[hh][:, V_SUM_LANE[hh % 2]:V_SUM_LANE[hh % 2] + 1] for hh in range(NH)]
        start = pl.multiple_of(qi_tab[t] * tq, tq)
        for pr in range(NH // 2):
            o_ref[0, pl.ds(start, tq), pr * HEAD_PAD:(pr + 1) * HEAD_PAD] = (
                jnp.where(lane < VDIM, o[2 * pr], o[2 * pr + 1]).astype(o_ref.dtype))

    def softmax(t, ss, ms, accs):
        fresh = kj_tab[t] == 0
        new_ms, new_accs, new_ps = [], [], []
        for hh in range(NH):
            m_in = jnp.where(fresh, NEG, ms[hh])
            m_new = jnp.maximum(m_in, jnp.max(ss[hh], axis=-1, keepdims=True))
            new_ps.append(jnp.exp2(ss[hh] - m_new).astype(BF16))
            new_ms.append(m_new)
            new_accs.append(jnp.exp2(m_in - m_new) * accs[hh])
        return new_ms, new_accs, new_ps

    def trip(t, state):
        ms, accs, ps = state
        ss = scores(t)
        accs = weigh(t - 1, accs, ps)
        state = softmax(t, ss, ms, accs)
        emit(t - 1, accs)
        return state

    def trips(i, state):
        for u in range(unroll):
            state = trip(1 + i * unroll + u, state)
        return state

    zero = jnp.zeros((tq, HEAD_PAD), F32)
    neg = jnp.full((tq, 1), NEG, F32)
    state = softmax(0, scores(0), [neg] * NH, [zero] * NH)
    looped = (n - 1) // unroll
    state = lax.fori_loop(0, looped, trips, state)
    for t in range(1 + looped * unroll, n):
        state = trip(t, state)
    emit(n - 1, weigh(n - 1, state[1], state[2]))


def _mla(q, k, v, *, tq, tk, unroll):
    B, S, _ = q.shape
    per = tq // tk
    steps = [(i, j) for i in range(S // tq) for j in range((i + 1) * per)]
    tab = lambda f: jnp.asarray([f(i, j) for i, j in steps], jnp.int32)
    qi_tab, kj_tab = tab(lambda i, j: i), tab(lambda i, j: j)
    bias_tab = tab(lambda i, j: max(0, j - i * per + 1))
    pair = lambda width: pl.BlockSpec((1, S, width), lambda b, p, *_: (b, 0, p))
    grid_spec = pltpu.PrefetchScalarGridSpec(
        num_scalar_prefetch=3, grid=(B, HEADS // MLA_HEADS_PER_STEP),
        in_specs=[pair(MLA_HEADS_PER_STEP * HEAD_PAD)] * 3,
        out_specs=pair(MLA_HEADS_PER_STEP * VDIM),
        scratch_shapes=[pltpu.VMEM((1 + per, tq, tk), F32)])
    return pl.pallas_call(
        functools.partial(_mla_kernel, tq=tq, tk=tk, n_steps=len(steps), unroll=unroll), grid_spec=grid_spec,
        out_shape=jax.ShapeDtypeStruct((B, S, HEADS * VDIM), BF16),
        compiler_params=_params("parallel", "parallel"), name="mla",
    )(qi_tab, kj_tab, bias_tab, q, k, v)


def _cpow(zr, zi, e):
    mag = jnp.exp(e * zr)
    return mag * jnp.cos(e * zi), mag * jnp.sin(e * zi)


def _s5_kernel(u_ref, ldt_ref, lrr_ref, lir_ref, lrc_ref, lic_ref, bx_ref, by_ref, cx_ref, cy_ref,
               y_ref, tt_ref, wst_ref, wout_ref, ep_ref, lhs_ref, *, nb, nk):
    P2 = 2 * SSM_STATE
    G = SSM_GROUP
    dt = jnp.exp(ldt_ref[0])
    lr = lrr_ref[0]
    li = lir_ref[0]
    zr = lr * dt
    zi = li * dt
    ar, ai = _cpow(zr, zi, 1.0)
    den = lr * lr + li * li
    cr = ((ar - 1.0) * lr + ai * li) / den
    ci = (ai * lr - (ar - 1.0) * li) / den
    bx = bx_ref[0]
    by = by_ref[0]
    bbx = cr * bx + ci * by
    bby = cr * by - ci * bx
    lane = lax.broadcasted_iota(jnp.int32, (1, P2), 1)
    sign_lane = jnp.where(lane < SSM_STATE, 1.0, -1.0).astype(F32)
    e_rows = (CHUNK - 1 - lax.broadcasted_iota(jnp.int32, (CHUNK, 1), 0)).astype(F32)
    pr, pi = _cpow(zr, zi, e_rows)
    for c in range(G):
        wst_ref[c * CHUNK:(c + 1) * CHUNK, :] = (pr * bbx[c:c + 1, :] + pi * bby[c:c + 1, :]).astype(BF16)
    zrc = lrc_ref[0] * dt
    zic = lic_ref[0] * dt
    tau = lax.broadcasted_iota(jnp.int32, (1, CHUNK), 1).astype(F32)
    h0r, h0i = _cpow(zrc[:SSM_STATE], zic[:SSM_STATE], tau)
    a0r = jnp.concatenate([h0r, h0r], axis=0)
    a0i = jnp.concatenate([h0i, h0i], axis=0)
    acr, aci = _cpow(zrc, zic, 1.0)
    a1r, a1i = a0r * acr - a0i * aci, a0r * aci + a0i * acr
    cx = cx_ref[0]
    cy = cy_ref[0]
    sub = lax.broadcasted_iota(jnp.int32, (P2, 1), 0)
    sign_sub = jnp.where(sub < SSM_STATE, 1.0, -1.0).astype(F32)
    for c in range(G):
        cxc = cx[:, c:c + 1]
        cyc = cy[:, c:c + 1]
        ep_ref[:, c * CHUNK:(c + 1) * CHUNK] = a0r * cxc + a0i * cyc
        wout_ref[:, c * CHUNK:(c + 1) * CHUNK] = ((a1r * cxc + a1i * cyc) * sign_sub).astype(BF16)
    km = jnp.dot(bbx * sign_lane, ep_ref[...], preferred_element_type=F32, precision=lax.Precision.HIGHEST)
    jj = lax.broadcasted_iota(jnp.int32, (CHUNK, CHUNK), 0)
    tcol = lax.broadcasted_iota(jnp.int32, (CHUNK, CHUNK), 1)
    lower = tcol >= jj

    for cp in range(G):
        rows_cp = slice(cp * CHUNK, (cp + 1) * CHUNK)
        for c in range(G):
            blk = jnp.broadcast_to(km[cp:cp + 1, c * CHUNK:(c + 1) * CHUNK], (CHUNK, CHUNK))
            blk = pltpu.roll(blk, 0, 1, stride=1, stride_axis=0)
            tt_ref[rows_cp, c * CHUNK:(c + 1) * CHUNK] = jnp.where(lower, blk, 0.0).astype(BF16)
        for b in range(nb):
            lhs_ref[b * nk:(b + 1) * nk, rows_cp] = u_ref[b, 0, pl.ds(cp, nk, stride=G), :].astype(BF16)
    lhs = lhs_ref[...]
    y = _dot(lhs, tt_ref[...])
    x = _dot(lhs, wst_ref[...])
    kidx = lax.broadcasted_iota(jnp.int32, (nb * nk, 1), 0) % nk
    qr, qi = ar, ai
    for _ in range(int(math.log2(CHUNK))):
        qr, qi = qr * qr - qi * qi, 2.0 * qr * qi
    d = 1
    while d < nk:
        sx = jnp.where(kidx >= d, pltpu.roll(x, d, 0), 0.0)
        x = x + qr * sx + (qi * -sign_lane) * pltpu.roll(sx, SSM_STATE, 1)
        qr, qi = qr * qr - qi * qi, 2.0 * qr * qi
        d *= 2
    x0 = jnp.where(kidx >= 1, pltpu.roll(x, 1, 0), 0.0)
    y = y + _dot(x0.astype(BF16), wout_ref[...])
    for b in range(nb):
        for c in range(G):
            y_ref[b, 0, pl.ds(c, nk, stride=G), :] = y[b * nk:(b + 1) * nk, c * CHUNK:(c + 1) * CHUNK]


def _s5(u4, ldt, lrr, lir, lrc, lic, bx, by, cx, cy, *, l):
    B, NG, R, _ = u4.shape
    nk = R // SSM_GROUP
    gspec = lambda arr: pl.BlockSpec((None, 1) + arr.shape[2:], lambda g: (l, g, 0, 0))
    ublk = pl.BlockSpec((B, 1, R, CHUNK), lambda g: (0, g, 0, 0))
    W = SSM_GROUP * CHUNK
    return pl.pallas_call(
        functools.partial(_s5_kernel, nb=B, nk=nk), grid=(NG,),
        in_specs=[ublk] + [gspec(a) for a in (ldt, lrr, lir, lrc, lic, bx, by, cx, cy)],
        out_specs=ublk,
        out_shape=jax.ShapeDtypeStruct(u4.shape, F32),
        scratch_shapes=[
            pltpu.VMEM((W, W), BF16),
            pltpu.VMEM((W, 2 * SSM_STATE), BF16),
            pltpu.VMEM((2 * SSM_STATE, W), BF16),
            pltpu.VMEM((2 * SSM_STATE, W), F32),
            pltpu.VMEM((B * nk, W), BF16),
        ],
        compiler_params=_params("parallel"), name="s5",
    )(u4, ldt, lrr, lir, lrc, lic, bx, by, cx, cy)


def _memkv_kernel(mem_ref, g_ref, w_ref, k_ref, v_ref):
    mn = _rms_rows(mem_ref[0], g_ref[...]).astype(BF16)
    kv = _dot(mn, w_ref[...])
    d = k_ref.shape[-1]
    k_ref[0] = kv[:, :d].astype(BF16)
    v_ref[0] = kv[:, d:].astype(BF16)


def _memkv(mem, g, w, *, l):
    B, M, D = mem.shape
    blk = pl.BlockSpec((1, M, D), lambda b: (b, 0, 0))
    return pl.pallas_call(
        _memkv_kernel, grid=(B,),
        in_specs=[blk, _layer_spec(g, l), _layer_spec(w, l)],
        out_specs=[blk, blk],
        out_shape=[jax.ShapeDtypeStruct((B, M, D), BF16)] * 2,
        compiler_params=_params("parallel"), name="memkv",
    )(mem, g, w)


def _mix_kernel(h_ref, a_ref, yt_ref, ut_ref, d_ref, wglut_ref, bglu_ref, ag_ref, sg_ref, woa_ref, wos_ref,
                xg_ref, wxq_ref, kx_ref, vx_ref, wxo_ref, o_ref):
    def channel_major(ref):
        n_ch = ref.shape[1] * SSM_GROUP
        return jnp.concatenate([ref[0, :, dk * SSM_GROUP:(dk + 1) * SSM_GROUP, :].reshape(n_ch, CHUNK)
                                for dk in range(ref.shape[2] // SSM_GROUP)], axis=1)

    y = channel_major(yt_ref) + d_ref[...] * channel_major(ut_ref)
    g = jax.nn.gelu(y)
    z = _dot(wglut_ref[...], g.astype(BF16)) + bglu_ref[...]
    s = y * jax.nn.sigmoid(z)
    s_n = s * lax.rsqrt(jnp.mean(s * s, axis=0, keepdims=True) + EPS) * sg_ref[...]
    a_n = _rms_rows(a_ref[0].astype(F32), ag_ref[...])
    h1 = h_ref[0] + _dot(a_n.astype(BF16), woa_ref[...]) + _dot(s_n.T.astype(BF16), wos_ref[...])
    qx = _dot(_rms_rows(h1, xg_ref[...]).astype(BF16), wxq_ref[...])
    dh = qx.shape[-1] // X_HEADS
    outs = []
    for hd in range(X_HEADS):
        sl = slice(hd * dh, (hd + 1) * dh)
        sc = _dot_nt(qx[:, sl].astype(BF16), kx_ref[0, :, sl]) * (dh ** -0.5)
        p = jnp.exp(sc - jnp.max(sc, axis=-1, keepdims=True))
        p = p / jnp.sum(p, axis=-1, keepdims=True)
        outs.append(_dot(p.astype(BF16), vx_ref[0, :, sl]))
    o = jnp.concatenate(outs, axis=-1)
    o_ref[0] = h1 + _dot(o.astype(BF16), wxo_ref[...])


def _mix(h, a_out, yt, ut, d, wglut, bglu, ag, sg, woa, wos, xg, wxq, kx, vx, wxo, *, ts, l):
    B, S, D = h.shape
    M = kx.shape[1]
    row = lambda b, i: (b, i, 0)
    colmaj = pl.BlockSpec((1, yt.shape[1], ts // CHUNK * SSM_GROUP, CHUNK), lambda b, i: (b, 0, i, 0))
    mem = pl.BlockSpec((1, M, D), lambda b, i: (b, 0, 0))
    consts = [d, wglut, bglu, ag, sg, woa, wos, xg, wxq]
    return pl.pallas_call(
        _mix_kernel, grid=(B, S // ts),
        in_specs=[pl.BlockSpec((1, ts, D), row), pl.BlockSpec((1, ts, a_out.shape[-1]), row), colmaj, colmaj]
                 + [_layer_spec(c, l, resident=True) for c in consts] + [mem, mem, _layer_spec(wxo, l, resident=True)],
        out_specs=pl.BlockSpec((1, ts, D), row),
        out_shape=jax.ShapeDtypeStruct((B, S, D), F32),
        compiler_params=_params("parallel", "parallel"), name="mix",
    )(h, a_out, yt, ut, *consts, kx, vx, wxo)


def _swiglu_kernel(h_ref, g_ref, wg_ref, wu_ref, wd_ref, fg_ref, o_ref, *, final_norm, chunks):
    h = h_ref[...]
    hn = _rms_rows(h, g_ref[...]).astype(BF16)
    out = h
    lo = 0
    for width in chunks:
        gate = _dot(hn, wg_ref[:, lo:lo + width])
        up = _dot(hn, wu_ref[:, lo:lo + width])
        out = out + _dot((gate * jax.nn.sigmoid(gate) * up).astype(BF16), wd_ref[lo:lo + width, :])
        lo += width
    if final_norm:
        out = _rms_rows(out, fg_ref[...])
    o_ref[...] = out


def _swiglu(h2d, g, wg, wu, wd, fg, *, tm, chunks, final_norm, l):
    T, D = h2d.shape
    return pl.pallas_call(
        functools.partial(_swiglu_kernel, final_norm=final_norm, chunks=chunks), grid=(T // tm,),
        in_specs=[pl.BlockSpec((tm, D), lambda i: (i, 0)), _layer_spec(g, l),
                  _layer_spec(wg, l, resident=True), _layer_spec(wu, l, resident=True),
                  _layer_spec(wd, l, resident=True), _const_spec(fg.shape)],
        out_specs=pl.BlockSpec((tm, D), lambda i: (i, 0)),
        out_shape=jax.ShapeDtypeStruct((T, D), F32),
        compiler_params=_params("parallel"), name="swiglu",
    )(h2d, g, wg, wu, wd, fg)


def _rope_swap(w):
    half = w.shape[-1] // 2
    return jnp.concatenate([-w[..., half:], w[..., :half]], axis=-1)


def _tile_of(d, prefs):
    for t in prefs:
        if d % t == 0:
            return t
    return d


def _split(total, target, align):
    n = -(-total // target)
    units = -(-total // align)
    widths = [(units // n + (1 if i < units % n else 0)) * align for i in range(n)]
    widths[-1] -= sum(widths) - total
    return tuple(w for w in widths if w > 0)


def kernel(x, mem, positions, norm_mix_g, w_in, q_norm_g, w_uq, kv_norm_g, w_ukv, ssm_lambda_re, ssm_lambda_im,
           ssm_log_dt, ssm_b_re, ssm_b_im, ssm_c_re, ssm_c_im, ssm_d, ssm_w_glu, ssm_b_glu, attn_out_g, ssm_out_g,
           w_out, norm_x_g, mem_norm_g, w_xq, w_xkv, w_xo, norm_ffn_g, w_gate, w_up, w_down, final_norm_g):
    B, S, D = x.shape
    depth = w_in.shape[0]
    n_ssm = ssm_d.shape[-1]
    n_groups = n_ssm // SSM_GROUP
    mla_w = HEADS * VDIM
    assert S % CHUNK == 0
    nk = S // CHUNK
    tm = _tile_of(S, (512, 256, 128))
    ff_chunks = _split(w_gate.shape[-1], 4 * MXU_WIDTH, MXU_WIDTH)

    pos3 = positions.reshape(B, S, 1)
    freqs = ROPE_THETA ** (-jnp.arange(0, ROPE, 2, dtype=F32) / ROPE)
    freq128 = jnp.zeros((1, LANES), F32).at[0, NOPE:NOPE + ROPE].set(jnp.concatenate([freqs, freqs]))
    zeros = lambda *s: jnp.zeros(s, F32)

    def layout(wi, wuq, wukv, lr, li, ldt, b_re, b_im, c_re, c_im, wglu, wo):
        w_kr = wi[:, Q_LORA + KV_LORA:Q_LORA + KV_LORA + ROPE]
        pad_rope = lambda w: jnp.concatenate([zeros(D, NOPE), w, zeros(D, HEAD_PAD - NOPE - ROPE)], axis=1)
        wcat = jnp.concatenate([wi[:, :Q_LORA + KV_LORA], pad_rope(w_kr), pad_rope(_rope_swap(w_kr))], axis=1).astype(BF16)
        wut = wi[:, Q_LORA + KV_LORA + ROPE:].T.astype(BF16)
        wq = wuq.reshape(Q_LORA, HEADS, NOPE + ROPE)
        zq = zeros(Q_LORA, HEADS, HEAD_PAD - NOPE - ROPE)
        wqm = jnp.concatenate([wq, zq], axis=-1).reshape(Q_LORA, HEADS * HEAD_PAD).astype(BF16)
        wqs = jnp.concatenate([zeros(Q_LORA, HEADS, NOPE), _rope_swap(wq[..., NOPE:]), zq], axis=-1)
        wqs = wqs.reshape(Q_LORA, HEADS * HEAD_PAD).astype(BF16)
        wkv = wukv.reshape(KV_LORA, HEADS, NOPE + VDIM)
        wk = jnp.concatenate([wkv[..., :NOPE], zeros(KV_LORA, HEADS, HEAD_PAD - NOPE)], axis=-1)
        wk = wk.reshape(KV_LORA, HEADS * HEAD_PAD).astype(BF16)
        wv2 = wkv[..., NOPE:].reshape(KV_LORA, HEADS // 2, 2, VDIM)
        zv = zeros(KV_LORA, HEADS // 2, HEAD_PAD - VDIM)
        wv = jnp.concatenate([wv2[:, :, 0], zv, zv, wv2[:, :, 1]], axis=-1)
        wv = wv.reshape(KV_LORA, HEADS * HEAD_PAD).astype(BF16)
        dup = lambda a: jnp.concatenate([a, a], axis=-1)
        brt, bit = jnp.swapaxes(b_re, 1, 2), jnp.swapaxes(b_im, 1, 2)
        crt, cit = jnp.swapaxes(c_re, 1, 2), jnp.swapaxes(c_im, 1, 2)
        s5p = (ldt.reshape(n_groups, 1, 1), dup(lr)[:, None, :], dup(li)[:, None, :], dup(lr)[:, :, None],
               dup(li)[:, :, None], jnp.concatenate([brt, bit], axis=-1), jnp.concatenate([-bit, brt], axis=-1),
               jnp.concatenate([crt, cit], axis=1), jnp.concatenate([-cit, crt], axis=1))
        return (wcat, wut, wqm, wqs, wk, wv), s5p, (wglu.T.astype(BF16), wo[:mla_w].astype(BF16),
                                                     wo[mla_w:].astype(BF16))

    front_w, s5_p, (wglut, woa, wos) = jax.vmap(layout)(
        w_in, w_uq, w_ukv, ssm_lambda_re, ssm_lambda_im, ssm_log_dt, ssm_b_re, ssm_b_im, ssm_c_re, ssm_c_im,
        ssm_w_glu, w_out)
    wcat, wut, wqm, wqs, wk, wv = front_w
    row = lambda g: g[:, None, :]
    col = lambda g: g[:, :, None]
    bf = lambda w: w.astype(BF16)
    wxq, wxkv, wxo, wg, wu, wd = bf(w_xq), bf(w_xkv), bf(w_xo), bf(w_gate), bf(w_up), bf(w_down)
    t_ff = _tile_of(B * S, (1024, 512, 256, 128))

    h = x
    for l in range(depth):
        q, k, v, ut = _front(h, pos3, freq128, row(norm_mix_g), wcat, wut, row(q_norm_g), wqm, wqs,
                             row(kv_norm_g), wk, wv, tm=_tile_of(S, (1024, 512, 256, 128)), l=l)
        a_out = _mla(q, k, v, tq=tm, tk=min(MLA_TK, tm), unroll=MLA_UNROLL)
        yt = _s5(ut, *s5_p, l=l)
        kx, vx = _memkv(mem, row(mem_norm_g), wxkv, l=l)
        h = _mix(h, a_out, yt, ut, col(ssm_d), wglut, col(ssm_b_glu), row(attn_out_g), col(ssm_out_g), woa, wos,
                 row(norm_x_g), wxq, kx, vx, wxo, ts=_tile_of(S, (1024, 512, 256, 128)), l=l)
        h = _swiglu(h.reshape(B * S, D), row(norm_ffn_g), wg, wu, wd, final_norm_g[None], tm=t_ff,
                    chunks=ff_chunks, final_norm=(l == depth - 1), l=l).reshape(B, S, D)
    return h
```
